```python
import jax, jax.numpy as jnp
from jax import lax
import numpy as np

D_MODEL = 2048
BATCH = 4
SEQ = 2048
DEPTH = 2

D_FF = 5632
SSD_HEADS = 32
SSD_HEAD_DIM = 64
SSD_D = SSD_HEADS * SSD_HEAD_DIM
SSD_GROUPS = 8
SSD_STATE = 128
SSD_CONV = 4
SSD_CHUNK = 128
SSD_XBC = SSD_D + 2 * SSD_GROUPS * SSD_STATE
RET_HEADS = 8
RET_QK_DIM = 256
RET_V_DIM = 256
RET_D = RET_HEADS * RET_V_DIM
RET_CHUNK = 128
RET_THETA = 10000.0
ATT_HEADS = 16
ATT_HEAD_DIM = 128
IDX_HEADS = 16
IDX_DIM = 64
TOPK_MAX = 256
Q_BLOCK = 128
ROPE_THETA = 500000.0
ROPE_FRACTION = 4
EPS = 1e-6

IN0_WIDTHS = (SSD_D, SSD_XBC, SSD_HEADS, RET_HEADS * RET_QK_DIM, RET_HEADS * RET_QK_DIM, RET_D, RET_D)
IN0 = sum(IN0_WIDTHS)
IN1_WIDTHS = (ATT_HEADS * ATT_HEAD_DIM, ATT_HEAD_DIM, ATT_HEAD_DIM, IDX_HEADS * IDX_DIM, IDX_DIM, IDX_HEADS)
IN1 = sum(IN1_WIDTHS)
N_EVEN = (DEPTH + 1) // 2
N_ODD = DEPTH // 2

kernel_name = "hybrid_ssd_retention_dsa_macaron_adaln"


def split_cols(a, widths):
    return jnp.split(a, [int(o) for o in np.cumsum(widths)[:-1]], axis=-1)


def rms_norm(x, w):
    xf = x.astype(jnp.float32)
    y = xf * lax.rsqrt(jnp.mean(xf * xf, axis=-1, keepdims=True) + EPS)
    return (y * w.astype(jnp.float32)).astype(x.dtype)


def rope(x, positions, rot_dim, theta):
    half = rot_dim // 2
    inv = theta ** (-jnp.arange(half, dtype=jnp.float32) / half)
    ang = positions.astype(jnp.float32)[..., None] * inv
    cos, sin = jnp.cos(ang)[:, :, None, :], jnp.sin(ang)[:, :, None, :]
    xr = x[..., :rot_dim].astype(jnp.float32)
    x1, x2 = xr[..., :half], xr[..., half:]
    rot = jnp.concatenate([x1 * cos - x2 * sin, x1 * sin + x2 * cos], axis=-1).astype(x.dtype)
    return jnp.concatenate([rot, x[..., rot_dim:]], axis=-1)


def swiglu(h, w_gate, w_up, w_down):
    return (jax.nn.silu(h @ w_gate) * (h @ w_up)) @ w_down


def causal_depthwise_conv(u, w, b):
    ch = u.shape[-1]
    out = lax.conv_general_dilated(u, w[:, None, :], (1,), [(w.shape[0] - 1, 0)],
                                   dimension_numbers=('NWC', 'WIO', 'NWC'), feature_group_count=ch)
    return out + b


def ssd_mixer(z, xbc, dt_raw, conv_w, conv_b, A_log, dt_bias, D_skip, norm_w):
    Bsz, L, _ = xbc.shape
    Q, G, R, P, N = SSD_CHUNK, SSD_GROUPS, SSD_HEADS // SSD_GROUPS, SSD_HEAD_DIM, SSD_STATE
    NC = L // Q
    xbc = jax.nn.silu(causal_depthwise_conv(xbc, conv_w, conv_b))
    xs, Bm, Cm = split_cols(xbc, (SSD_D, G * N, G * N))
    xs = xs.reshape(Bsz, NC, Q, G, R, P)
    Bm = Bm.reshape(Bsz, NC, Q, G, N)
    Cm = Cm.reshape(Bsz, NC, Q, G, N)
    dt = jax.nn.softplus(dt_raw.astype(jnp.float32) + dt_bias.astype(jnp.float32))
    A = -jnp.exp(A_log.astype(jnp.float32))
    dA = (dt * A).reshape(Bsz, NC, Q, G, R)
    dt = dt.reshape(Bsz, NC, Q, G, R)
    cum = jnp.cumsum(dA, axis=2)
    xdt = xs * dt[..., None]
    seg = cum[:, :, :, None] - cum[:, :, None, :]
    causal = jnp.tril(jnp.ones((Q, Q), dtype=bool))[:, :, None, None]
    decay = jnp.exp(jnp.where(causal, seg, -jnp.inf))
    CB = jnp.einsum('bclgn,bcsgn->bclsg', Cm, Bm)
    y_diag = jnp.einsum('bclsg,bclsgr,bcsgrp->bclgrp', CB, decay, xdt)
    decay_s = jnp.exp(cum[:, :, -1:] - cum)
    states = jnp.einsum('bcsgn,bcsgr,bcsgrp->bcgrpn', Bm, decay_s, xdt)
    chunk_decay = jnp.exp(cum[:, :, -1])

    def step(h, inp):
        s_c, a_c = inp
        return h * a_c[..., None, None] + s_c, h

    h0 = jnp.zeros((Bsz, G, R, P, N), jnp.float32)
    _, h_prev = lax.scan(step, h0, (jnp.moveaxis(states, 1, 0), jnp.moveaxis(chunk_decay, 1, 0)))
    h_prev = jnp.moveaxis(h_prev, 0, 1)
    y_off = jnp.einsum('bclgn,bcgrpn,bclgr->bclgrp', Cm, h_prev, jnp.exp(cum))
    y = y_diag + y_off + xs * D_skip.reshape(G, R)[:, :, None].astype(jnp.float32)
    y = y.reshape(Bsz, L, SSD_D)
    y = rms_norm(y * jax.nn.silu(z.astype(jnp.float32)), norm_w)
    return y.astype(z.dtype)


def retention_mixer(q, k, v, g, positions, norm_w):
    Bsz, L, _ = q.shape
    H, Q = RET_HEADS, RET_CHUNK
    NC = L // Q
    q = rope(q.reshape(Bsz, L, H, RET_QK_DIM), positions, RET_QK_DIM, RET_THETA)
    k = rope(k.reshape(Bsz, L, H, RET_QK_DIM), positions, RET_QK_DIM, RET_THETA) * (RET_QK_DIM ** -0.5)
    v = v.reshape(Bsz, L, H, RET_V_DIM)
    q = q.reshape(Bsz, NC, Q, H, RET_QK_DIM)
    k = k.reshape(Bsz, NC, Q, H, RET_QK_DIM)
    v = v.reshape(Bsz, NC, Q, H, RET_V_DIM)
    log_gamma = jnp.log(1.0 - 2.0 ** (-5.0 - jnp.arange(H, dtype=jnp.float32)))
    idx = jnp.arange(Q, dtype=jnp.float32)
    dist = idx[:, None] - idx[None, :]
    intra = jnp.exp(jnp.where(dist[..., None] >= 0, dist[..., None] * log_gamma, -jnp.inf))
    scores = jnp.einsum('bclhd,bcshd->bchls', q, k) * jnp.transpose(intra, (2, 0, 1))
    y_intra = jnp.einsum('bchls,bcshv->bclhv', scores, v)
    k_dec = jnp.exp((Q - 1 - idx)[:, None] * log_gamma)
    states = jnp.einsum('bcshd,sh,bcshv->bchdv', k, k_dec, v)
    chunk_decay = jnp.exp(Q * log_gamma)

    def step(r, s_c):
        return r * chunk_decay[:, None, None] + s_c, r

    r0 = jnp.zeros((Bsz, H, RET_QK_DIM, RET_V_DIM), jnp.float32)
    _, r_prev = lax.scan(step, r0, jnp.moveaxis(states, 1, 0))
    r_prev = jnp.moveaxis(r_prev, 0, 1)
    q_dec = jnp.exp((idx + 1)[:, None] * log_gamma)
    y_inter = jnp.einsum('bclhd,lh,bchdv->bclhv', q, q_dec, r_prev)
    y = (y_intra + y_inter).reshape(Bsz, L, H, RET_V_DIM)
    y = rms_norm(y, norm_w.reshape(H, RET_V_DIM)).reshape(Bsz, L, RET_D)
    return (jax.nn.silu(g.astype(jnp.float32)) * y).astype(g.dtype)


def dsa_mixer(proj, positions, idx_k_norm_w):
    Bsz, L, _ = proj.shape
    q, k, v, qi, ki, wi = split_cols(proj, IN1_WIDTHS)
    rot = ATT_HEAD_DIM // ROPE_FRACTION
    rot_i = IDX_DIM // ROPE_FRACTION
    q = rope(q.reshape(Bsz, L, ATT_HEADS, ATT_HEAD_DIM), positions, rot, ROPE_THETA)
    k = rope(k[:, :, None, :], positions, rot, ROPE_THETA)[:, :, 0]
    qi = rope(qi.reshape(Bsz, L, IDX_HEADS, IDX_DIM), positions, rot_i, ROPE_THETA)
    ki = rope(rms_norm(ki, idx_k_norm_w)[:, :, None, :], positions, rot_i, ROPE_THETA)[:, :, 0]
    wi = wi * (IDX_HEADS ** -0.5) * (IDX_DIM ** -0.5)
    kv = jnp.concatenate([k, v], axis=-1)
    topk = min(TOPK_MAX, L // 4)
    nb = L // Q_BLOCK
    key_pos = jnp.arange(L)

    def to_blocks(a):
        return jnp.swapaxes(a.reshape(Bsz, nb, Q_BLOCK, *a.shape[2:]), 0, 1)

    def block(args):
        qb, qib, wib, start = args
        q_pos = start + jnp.arange(Q_BLOCK)
        s = jax.nn.relu(jnp.einsum('bqhd,bsd->bqhs', qib, ki).astype(jnp.float32))
        score = jnp.einsum('bqhs,bqh->bqs', s, wib.astype(jnp.float32))
        visible = key_pos[None, :] <= q_pos[:, None]
        score = jnp.where(visible[None], score, -jnp.inf)
        _, sel = lax.top_k(score, topk)
        kv_sel = jax.vmap(lambda kvb, ib: kvb[ib])(kv, sel)
        k_sel, v_sel = kv_sel[..., :ATT_HEAD_DIM], kv_sel[..., ATT_HEAD_DIM:]
        logits = jnp.einsum('bqhd,bqkd->bqhk', qb, k_sel).astype(jnp.float32) * (ATT_HEAD_DIM ** -0.5)
        valid = sel <= q_pos[None, :, None]
        logits = jnp.where(valid[:, :, None, :], logits, -jnp.inf)
        p = jax.nn.softmax(logits, axis=-1).astype(v_sel.dtype)
        return jnp.einsum('bqhk,bqkd->bqhd', p, v_sel)

    starts = jnp.arange(nb) * Q_BLOCK
    out = lax.map(block, (to_blocks(q), to_blocks(qi), to_blocks(wi), starts))
    return jnp.swapaxes(out, 0, 1).reshape(Bsz, L, ATT_HEADS * ATT_HEAD_DIM)


def modulated_sublayer(x, mods, j, pre_w, post_w, fn, res_weight):
    shift, scale, gate = (mods[..., (3 * j + m) * D_MODEL:(3 * j + m + 1) * D_MODEL] for m in range(3))
    h = rms_norm(x, pre_w) * (1 + scale) + shift
    y = rms_norm(fn(h), post_w)
    return x + res_weight * gate * y


def setup_inputs(seed: int = 0) -> dict:
    key = jax.random.key(seed)
    ks = jax.random.split(key, 24)
    f32 = jnp.float32

    def nrm(k, shape, fan_in, gain=1.0):
        return jax.random.normal(k, shape, f32) * (gain * fan_in ** -0.5)

    def gain(k, shape):
        return 1.0 + 0.05 * jax.random.normal(k, shape, f32)

    x = jax.random.normal(ks[0], (BATCH, SEQ, D_MODEL), f32)
    c = jax.random.normal(ks[1], (BATCH, D_MODEL), f32)
    offset = jax.random.randint(ks[2], (BATCH, 1), 0, 4096, dtype=jnp.int32)
    positions = (jnp.arange(SEQ, dtype=jnp.int32)[None, :] + offset).astype(jnp.int32)
    mod_w = nrm(ks[3], (DEPTH, D_MODEL, 9 * D_MODEL), D_MODEL, 0.5)
    mod_b = 0.01 * jax.random.normal(ks[4], (DEPTH, 9 * D_MODEL), f32)
    norm_w = gain(ks[5], (DEPTH, 6, D_MODEL))
    ffn_w_gate = nrm(ks[6], (DEPTH, 2, D_MODEL, D_FF), D_MODEL)
    ffn_w_up = nrm(ks[7], (DEPTH, 2, D_MODEL, D_FF), D_MODEL)
    ffn_w_down = nrm(ks[8], (DEPTH, 2, D_FF, D_MODEL), D_FF)
    hy_w_in = nrm(ks[9], (N_EVEN, D_MODEL, IN0), D_MODEL)
    hy_conv_w = nrm(ks[10], (N_EVEN, SSD_CONV, SSD_XBC), SSD_CONV)
    hy_conv_b = 0.05 * jax.random.normal(ks[11], (N_EVEN, SSD_XBC), f32)
    ssd_A_log = jnp.log(jax.random.uniform(ks[12], (N_EVEN, SSD_HEADS), f32, 1.0, 16.0))
    dt0 = jnp.exp(jax.random.uniform(ks[13], (N_EVEN, SSD_HEADS), f32, np.log(1e-3), np.log(1e-1)))
    ssd_dt_bias = dt0 + jnp.log(-jnp.expm1(-dt0))
    ssd_D = 1.0 + 0.1 * jax.random.normal(ks[14], (N_EVEN, SSD_HEADS), f32)
    ssd_norm_w = gain(ks[15], (N_EVEN, SSD_D))
    ret_norm_w = gain(ks[16], (N_EVEN, RET_D))
    hy_w_out = nrm(ks[17], (N_EVEN, SSD_D + RET_D, D_MODEL), SSD_D + RET_D)
    dsa_w_in = nrm(ks[18], (N_ODD, D_MODEL, IN1), D_MODEL)
    idx_k_norm_w = gain(ks[19], (N_ODD, IDX_DIM))
    dsa_w_out = nrm(ks[20], (N_ODD, ATT_HEADS * ATT_HEAD_DIM, D_MODEL), ATT_HEADS * ATT_HEAD_DIM)
    return {"x": x, "c": c, "positions": positions, "mod_w": mod_w, "mod_b": mod_b, "norm_w": norm_w,
            "ffn_w_gate": ffn_w_gate, "ffn_w_up": ffn_w_up, "ffn_w_down": ffn_w_down,
            "hy_w_in": hy_w_in, "hy_conv_w": hy_conv_w, "hy_conv_b": hy_conv_b, "ssd_A_log": ssd_A_log,
            "ssd_dt_bias": ssd_dt_bias, "ssd_D": ssd_D, "ssd_norm_w": ssd_norm_w, "ret_norm_w": ret_norm_w,
            "hy_w_out": hy_w_out, "dsa_w_in": dsa_w_in, "idx_k_norm_w": idx_k_norm_w, "dsa_w_out": dsa_w_out}


def reference(x, c, positions, mod_w, mod_b, norm_w, ffn_w_gate, ffn_w_up, ffn_w_down,
              hy_w_in, hy_conv_w, hy_conv_b, ssd_A_log, ssd_dt_bias, ssd_D, ssd_norm_w, ret_norm_w,
              hy_w_out, dsa_w_in, idx_k_norm_w, dsa_w_out):
    cond = jax.nn.silu(c)
    for i in range(DEPTH):
        mods = (cond @ mod_w[i] + mod_b[i])[:, None, :]

        def ffn(h, f, i=i):
            return swiglu(h, ffn_w_gate[i, f], ffn_w_up[i, f], ffn_w_down[i, f])

        if i % 2 == 0:
            e = i // 2

            def mixer(h, e=e):
                proj = h @ hy_w_in[e]
                z, xbc, dt_raw, rq, rk, rv, rg = split_cols(proj, IN0_WIDTHS)
                ya = ssd_mixer(z, xbc, dt_raw, hy_conv_w[e], hy_conv_b[e], ssd_A_log[e],
                               ssd_dt_bias[e], ssd_D[e], ssd_norm_w[e])
                yb = retention_mixer(rq, rk, rv, rg, positions, ret_norm_w[e])
                return jnp.concatenate([ya.astype(h.dtype), yb.astype(h.dtype)], axis=-1) @ hy_w_out[e]
        else:
            o = i // 2

            def mixer(h, o=o):
                return dsa_mixer(h @ dsa_w_in[o], positions, idx_k_norm_w[o]) @ dsa_w_out[o]

        x = modulated_sublayer(x, mods, 0, norm_w[i, 0], norm_w[i, 1], lambda h: ffn(h, 0), 0.5)
        x = modulated_sublayer(x, mods, 1, norm_w[i, 2], norm_w[i, 3], mixer, 1.0)
        x = modulated_sublayer(x, mods, 2, norm_w[i, 4], norm_w[i, 5], lambda h: ffn(h, 1), 0.5)
    return x
```

```python
import functools
import math

import jax
import jax.numpy as jnp
from jax import lax
from jax.experimental import pallas as pl
from jax.experimental.pallas import tpu as pltpu

F32 = jnp.float32
BF16 = jnp.bfloat16
I32 = jnp.int32

EPS = 1e-6
SSD_HEADS = 32
SSD_HEAD_DIM = 64
SSD_GROUPS = 8
SSD_STATE = 128
SSD_CONV = 4
CHUNK = 128
RET_HEADS = 8
RET_DIM = 256
RET_THETA = 10000.0
ATT_HEADS = 16
ATT_DIM = 128
IDX_HEADS = 16
IDX_DIM = 64
TOPK = 256
Q_BLOCK = 128
ROPE_THETA = 500000.0
ROPE_FRACTION = 4

V7X_VMEM_BYTES = 64 * 1024 * 1024
VMEM_LIMIT = V7X_VMEM_BYTES - 8 * 1024 * 1024
LANES = 128
SUBLANES = 8

INT_MIN = -(2 ** 31)
INT_MAX = 2 ** 31 - 1
NEG_BIG = -1e30


def _cparams(sem):
    return pltpu.CompilerParams(dimension_semantics=sem, vmem_limit_bytes=VMEM_LIMIT)


def _sigmoid(x):
    return 1.0 / (1.0 + jnp.exp(-x))


def _silu(x):
    return x * _sigmoid(x)


def _softplus(x):
    return jnp.maximum(x, 0.0) + jnp.log1p(jnp.exp(-jnp.abs(x)))


def _rms(x, w):
    return x * lax.rsqrt(jnp.mean(x * x, axis=-1, keepdims=True) + EPS) * w


def _dot(a, b):
    return jnp.dot(a, b, preferred_element_type=F32)


def _dot_nt(a, b):
    return lax.dot_general(a, b, (((1,), (1,)), ((), ())), preferred_element_type=F32)


def _dot_tn(a, b):
    return lax.dot_general(a, b, (((0,), (0,)), ((), ())), preferred_element_type=F32)


def _split3(x):
    hi = x.astype(BF16)
    r = x - hi.astype(F32)
    mid = r.astype(BF16)
    lo = (r - mid.astype(F32)).astype(BF16)
    return hi, mid, lo


def _row_chunks(n_rows, rc, fn):
    def body(i, carry):
        fn(pl.ds(pl.multiple_of(i * rc, rc), rc))
        return carry
    lax.fori_loop(0, n_rows // rc, body, 0)


def _mods_kernel(c_ref, w_ref, b_ref, o_ref):
    cond = _silu(c_ref[...]).astype(BF16)
    o_ref[...] = _dot(cond, w_ref[...].astype(BF16)) + b_ref[...]


def _mods(c, mod_w, mod_b):
    depth, d, n = mod_w.shape
    bsz = c.shape[0]
    c8 = jnp.zeros((SUBLANES, d), F32).at[:bsz].set(c)
    tn = 1024
    out = pl.pallas_call(
        _mods_kernel,
        out_shape=jax.ShapeDtypeStruct((depth, SUBLANES, n), F32),
        grid=(depth, n // tn),
        in_specs=[
            pl.BlockSpec((SUBLANES, d), lambda i, j: (0, 0)),
            pl.BlockSpec((None, d, tn), lambda i, j: (i, 0, j)),
            pl.BlockSpec((None, 1, tn), lambda i, j: (i, 0, j)),
        ],
        out_specs=pl.BlockSpec((None, SUBLANES, tn), lambda i, j: (i, 0, j)),
        compiler_params=_cparams(("parallel", "parallel")),
        name="mods",
    )(c8, mod_w, mod_b.reshape(depth, 1, n))
    return out[:, :bsz].reshape(depth, bsz * 3, 3, d)


def _norm_mod_kernel(x_ref, w_ref, m_ref, o_ref):
    h = _rms(x_ref[...], w_ref[...]) * (1.0 + m_ref[1:2, :]) + m_ref[0:1, :]
    o_ref[...] = h.astype(o_ref.dtype)


def _norm_mod(x2, pre_w, mods_l, sub, seq):
    t, d = x2.shape
    tm = 256
    per_b = seq // tm
    return pl.pallas_call(
        _norm_mod_kernel,
        out_shape=jax.ShapeDtypeStruct((t, d), BF16),
        grid=(t // tm,),
        in_specs=[
            pl.BlockSpec((tm, d), lambda i: (i, 0)),
            pl.BlockSpec((1, d), lambda i: (0, 0)),
            pl.BlockSpec((None, 3, d), lambda i: ((i // per_b) * 3 + sub, 0, 0)),
        ],
        out_specs=pl.BlockSpec((tm, d), lambda i: (i, 0)),
        compiler_params=_cparams(("parallel",)),
        name="norm_mod",
    )(x2, pre_w.reshape(1, d), mods_l)


MM_TM = 1024


def _mm_kernel(a_ref, w_ref, o_ref):
    o_ref[...] = _dot(a_ref[...], w_ref[...].astype(BF16)).astype(o_ref.dtype)


def _matmul(a, w, out_dtype, tm, tn, n_out=None, w_col_block0=0):
    m, k = a.shape
    n_out = w.shape[1] if n_out is None else n_out
    return pl.pallas_call(
        _mm_kernel,
        out_shape=jax.ShapeDtypeStruct((m, n_out), out_dtype),
        grid=(m // tm, n_out // tn),
        in_specs=[
            pl.BlockSpec((tm, k), lambda i, j: (i, 0)),
            pl.BlockSpec((k, tn), lambda i, j: (0, j + w_col_block0)),
        ],
        out_specs=pl.BlockSpec((tm, tn), lambda i, j: (i, j)),
        compiler_params=_cparams(("parallel", "parallel")),
        name="matmul",
    )(a, w)


FFN_TM = 1024
FFN_TF = 256
ROW_CHUNK = 128


def _ffn_kernel(x_ref, m_ref, prew_ref, postw_ref, wg_ref, wu_ref, wd_ref, o_ref, h_ref, *, nf, tm):
    f = pl.program_id(1)

    @pl.when(f == 0)
    def _():
        def pro(rows):
            h = _rms(x_ref[rows, :], prew_ref[...]) * (1.0 + m_ref[1:2, :]) + m_ref[0:1, :]
            h_ref[rows, :] = h.astype(BF16)
        _row_chunks(tm, ROW_CHUNK, pro)

    h = h_ref[...]
    g = _dot(h, wg_ref[...].astype(BF16))
    u = _dot(h, wu_ref[...].astype(BF16))
    a = (_silu(g) * u).astype(BF16)
    d = _dot(a, wd_ref[...].astype(BF16))

    @pl.when(f == 0)
    def _():
        o_ref[...] = d

    @pl.when(f > 0)
    def _():
        o_ref[...] += d

    @pl.when(f == nf - 1)
    def _():
        def epi(rows):
            y = _rms(o_ref[rows, :], postw_ref[...])
            o_ref[rows, :] = x_ref[rows, :] + 0.5 * m_ref[2:3, :] * y
        _row_chunks(tm, ROW_CHUNK, epi)


def _ffn(x2, mods_l, sub, pre_w, post_w, w_gate, w_up, w_down, seq):
    t, d = x2.shape
    dff = w_gate.shape[1]
    tm, tf = min(FFN_TM, seq), FFN_TF
    nf = dff // tf
    per_b = seq // tm
    one = pl.Buffered(1)
    return pl.pallas_call(
        functools.partial(_ffn_kernel, nf=nf, tm=tm),
        out_shape=jax.ShapeDtypeStruct((t, d), F32),
        grid=(t // tm, nf),
        in_specs=[
            pl.BlockSpec((tm, d), lambda i, f: (i, 0), pipeline_mode=one),
            pl.BlockSpec((None, 3, d), lambda i, f: ((i // per_b) * 3 + sub, 0, 0)),
            pl.BlockSpec((1, d), lambda i, f: (0, 0)),
            pl.BlockSpec((1, d), lambda i, f: (0, 0)),
            pl.BlockSpec((d, tf), lambda i, f: (0, f)),
            pl.BlockSpec((d, tf), lambda i, f: (0, f)),
            pl.BlockSpec((tf, d), lambda i, f: (f, 0)),
        ],
        out_specs=pl.BlockSpec((tm, d), lambda i, f: (i, 0)),
        scratch_shapes=[pltpu.VMEM((tm, d), BF16)],
        compiler_params=_cparams(("parallel", "arbitrary")),
        name="ffn",
    )(x2, mods_l, pre_w.reshape(1, d), post_w.reshape(1, d), w_gate, w_up, w_down)


PROJ_TM = 512
PROJ_TK = 1024


def _proj_resid_kernel(*refs, n_src, nk_src, nk, tm):
    a_refs = refs[:n_src]
    w_ref, x_ref, m_ref, postw_ref, o_ref = refs[n_src:]
    k = pl.program_id(1)
    a = a_refs[0][...]
    for s in range(1, n_src):
        a = jnp.where(k >= s * nk_src, a_refs[s][...], a)
    d = _dot(a, w_ref[...].astype(BF16))

    @pl.when(k == 0)
    def _():
        o_ref[...] = d

    @pl.when(k > 0)
    def _():
        o_ref[...] += d

    @pl.when(k == nk - 1)
    def _():
        def epi(rows):
            y = _rms(o_ref[rows, :], postw_ref[...])
            o_ref[rows, :] = x_ref[rows, :] + m_ref[2:3, :] * y
        _row_chunks(tm, ROW_CHUNK, epi)


def _proj_resid(srcs, w, x2, mods_l, sub, post_w, seq):
    t, d = x2.shape
    n_src = len(srcs)
    ksrc = srcs[0].shape[1]
    tm, tk = PROJ_TM, PROJ_TK
    nk_src = ksrc // tk
    nk = n_src * nk_src
    per_b = seq // tm
    one = pl.Buffered(1)

    def a_spec(s):
        return pl.BlockSpec((tm, tk), lambda i, k: (i, jnp.clip(k - s * nk_src, 0, nk_src - 1)))

    return pl.pallas_call(
        functools.partial(_proj_resid_kernel, n_src=n_src, nk_src=nk_src, nk=nk, tm=tm),
        out_shape=jax.ShapeDtypeStruct((t, d), F32),
        grid=(t // tm, nk),
        in_specs=[a_spec(s) for s in range(n_src)] + [
            pl.BlockSpec((tk, d), lambda i, k: (k, 0)),
            pl.BlockSpec((tm, d), lambda i, k: (i, 0), pipeline_mode=one),
            pl.BlockSpec((None, 3, d), lambda i, k: ((i // per_b) * 3 + sub, 0, 0)),
            pl.BlockSpec((1, d), lambda i, k: (0, 0)),
        ],
        out_specs=pl.BlockSpec((tm, d), lambda i, k: (i, 0)),
        compiler_params=_cparams(("parallel", "arbitrary")),
        name="proj_resid",
    )(*srcs, w, x2, mods_l, post_w.reshape(1, d))


SSD_D = SSD_HEADS * SSD_HEAD_DIM
SSD_GW = SSD_D // SSD_GROUPS
SSD_BCW = SSD_GROUPS * SSD_STATE
TAIL = SUBLANES


def _ssd_kernel(z_ref, xs_ref, bc_ref, dt_ref, dtT_ref, cw_ref, cb_ref, alr_ref, alc_ref, dbr_ref, dbc_ref,
                dexp_ref, nw_ref, e_ref, o_ref, s_ref, ubx_ref, ubc_ref):
    c = pl.program_id(1)
    q = CHUNK

    @pl.when(c == 0)
    def _():
        s_ref[...] = jnp.zeros_like(s_ref)
        ubx_ref[0:TAIL, :] = jnp.zeros((TAIL, SSD_D), F32)
        ubc_ref[0:TAIL, :] = jnp.zeros((TAIL, 2 * SSD_BCW), F32)

    ubx_ref[TAIL:TAIL + q, :] = xs_ref[...].astype(F32)
    ubc_ref[TAIL:TAIL + q, :] = bc_ref[...].astype(F32)

    def conv(ub_ref, lo, hi):
        acc = cb_ref[:, lo:hi]
        for k in range(SSD_CONV):
            off = TAIL - (SSD_CONV - 1) + k
            acc = acc + cw_ref[k:k + 1, lo:hi] * ub_ref[off:off + q, :]
        return _silu(acc)

    xs = conv(ubx_ref, 0, SSD_D)
    bc = conv(ubc_ref, SSD_D, SSD_D + 2 * SSD_BCW)
    ubx_ref[0:TAIL, :] = ubx_ref[q:q + TAIL, :]
    ubc_ref[0:TAIL, :] = ubc_ref[q:q + TAIL, :]

    rows = lax.broadcasted_iota(I32, (q, q), 0)
    cols = lax.broadcasted_iota(I32, (q, q), 1)
    causal = rows >= cols
    tril = causal.astype(BF16)
    triu = (rows <= cols).astype(BF16)

    dt = _softplus(dt_ref[:, 0:SSD_HEADS] + dbr_ref[...])
    dA = dt * (-jnp.exp(alr_ref[...]))
    cum = sum(_dot(tril, p) for p in _split3(dA))
    dtT = _softplus(dtT_ref[...] + dbc_ref[...])
    dAT = dtT * (-jnp.exp(alc_ref[...]))
    cumT = sum(_dot(p, triu) for p in _split3(dAT))

    ecum = jnp.exp(cum)
    decs = jnp.exp(cum[q - 1:q, :] - cum)
    stack = jnp.concatenate([dt, ecum, decs], axis=0)
    ex = sum(_dot(p, e_ref[...]) for p in _split3(stack))
    dt_x, ecum_x, decs_x = ex[0:q], ex[q:2 * q], ex[2 * q:3 * q]
    cdec_x = ecum_x[q - 1:q, :]

    xdt = xs * dt_x
    xw = (xdt * decs_x).astype(BF16)
    xdt_b = xdt.astype(BF16)

    y_parts = []
    for g in range(SSD_GROUPS):
        b_g = bc[:, g * SSD_STATE:(g + 1) * SSD_STATE].astype(BF16)
        c_g = bc[:, SSD_BCW + g * SSD_STATE:SSD_BCW + (g + 1) * SSD_STATE].astype(BF16)
        gs = slice(g * SSD_GW, (g + 1) * SSD_GW)
        cb = _dot_nt(c_g, b_g)
        s_g = s_ref[g]
        y_g = _dot(c_g, s_g.astype(BF16)) * ecum_x[:, gs]
        diag = []
        for r in range(SSD_HEADS // SSD_GROUPS):
            hh = g * (SSD_HEADS // SSD_GROUPS) + r
            seg = cum[:, hh:hh + 1] - cumT[hh:hh + 1, :]
            dec = jnp.where(causal, jnp.exp(seg), 0.0)
            m = (cb * dec).astype(BF16)
            hs = slice(g * SSD_GW + r * SSD_HEAD_DIM, g * SSD_GW + (r + 1) * SSD_HEAD_DIM)
            diag.append(_dot(m, xdt_b[:, hs]))
        y_parts.append(y_g + jnp.concatenate(diag, axis=1))
        s_ref[g] = s_g * cdec_x[:, gs] + _dot_tn(b_g, xw[:, gs])
    y = jnp.concatenate(y_parts, axis=1) + xs * dexp_ref[...]
    o_ref[...] = _rms(y * _silu(z_ref[...].astype(F32)), nw_ref[...]).astype(o_ref.dtype)


def _ssd(proj0, dt_pad, dtT, conv_w, conv_b, a_log, dt_bias, d_skip, norm_w, bsz, seq):
    t = proj0.shape[0]
    nc = seq // CHUNK
    h = SSD_HEADS
    expand = jnp.repeat(jnp.eye(h, dtype=BF16), SSD_HEAD_DIM, axis=1)
    dexp = jnp.repeat(d_skip.astype(F32), SSD_HEAD_DIM).reshape(1, SSD_D)
    full = lambda shape: pl.BlockSpec(shape, lambda b, c: (0,) * len(shape))
    col = lambda j: pl.BlockSpec((CHUNK, SSD_D), lambda b, c: (b * nc + c, j))
    return pl.pallas_call(
        _ssd_kernel,
        out_shape=jax.ShapeDtypeStruct((t, SSD_D), BF16),
        grid=(bsz, nc),
        in_specs=[
            col(0), col(1), col(2),
            pl.BlockSpec((CHUNK, LANES), lambda b, c: (b * nc + c, 0)),
            pl.BlockSpec((None, h, CHUNK), lambda b, c: (b, 0, c)),
            full((SSD_CONV, SSD_D + 2 * SSD_BCW)), full((1, SSD_D + 2 * SSD_BCW)),
            full((1, h)), full((h, 1)), full((1, h)), full((h, 1)),
            full((1, SSD_D)), full((1, SSD_D)), full((h, SSD_D)),
        ],
        out_specs=pl.BlockSpec((CHUNK, SSD_D), lambda b, c: (b * nc + c, 0)),
        scratch_shapes=[
            pltpu.VMEM((SSD_GROUPS, SSD_STATE, SSD_GW), F32),
            pltpu.VMEM((CHUNK + TAIL, SSD_D), F32),
            pltpu.VMEM((CHUNK + TAIL, 2 * SSD_BCW), F32),
        ],
        compiler_params=_cparams(("parallel", "arbitrary")),
        name="ssd",
    )(proj0, proj0, proj0, dt_pad, dtT, conv_w, conv_b.reshape(1, -1),
      a_log.reshape(1, h), a_log.reshape(h, 1), dt_bias.reshape(1, h), dt_bias.reshape(h, 1),
      dexp, norm_w.reshape(1, SSD_D), expand)


RET_D = RET_HEADS * RET_DIM


def _ret_kernel(q_ref, k_ref, v_ref, g_ref, pos_ref, inv_ref, nw_ref, o_ref, r_ref):
    c = pl.program_id(1)
    q = CHUNK
    half = RET_DIM // 2

    @pl.when(c == 0)
    def _():
        r_ref[...] = jnp.zeros_like(r_ref)

    ang = pos_ref[...].astype(F32) * inv_ref[...]
    cs, sn = jnp.cos(ang), jnp.sin(ang)
    li = lax.broadcasted_iota(I32, (q, q), 0)
    si = lax.broadcasted_iota(I32, (q, q), 1)
    dist = (li - si).astype(F32)
    lcol = lax.broadcasted_iota(I32, (q, 1), 0).astype(F32)

    def rope(x):
        x1, x2 = x[:, :half], x[:, half:]
        return jnp.concatenate([x1 * cs - x2 * sn, x1 * sn + x2 * cs], axis=1)

    for h in range(RET_HEADS):
        lg = math.log(1.0 - 2.0 ** (-5.0 - h))
        hs = slice(h * RET_DIM, (h + 1) * RET_DIM)
        qr = rope(q_ref[:, hs].astype(F32)).astype(BF16)
        kr = rope(k_ref[:, hs].astype(F32)) * (RET_DIM ** -0.5)
        v = v_ref[:, hs]
        intra = jnp.where(dist >= 0.0, jnp.exp(dist * lg), 0.0)
        sc = _dot_nt(qr, kr.astype(BF16)) * intra
        r_h = r_ref[h]
        y = _dot(sc.astype(BF16), v) + _dot(qr, r_h.astype(BF16)) * jnp.exp((lcol + 1.0) * lg)
        kd = (kr * jnp.exp((q - 1.0 - lcol) * lg)).astype(BF16)
        r_ref[h] = r_h * math.exp(q * lg) + _dot_tn(kd, v)
        gate = _silu(g_ref[:, hs].astype(F32))
        o_ref[:, hs] = (gate * _rms(y, nw_ref[:, hs])).astype(o_ref.dtype)


def _retention(proj0, pos_col, norm_w, bsz, seq):
    t = proj0.shape[0]
    nc = seq // CHUNK
    half = RET_DIM // 2
    inv = (RET_THETA ** (-jnp.arange(half, dtype=F32) / half)).reshape(1, half)
    col = lambda j: pl.BlockSpec((CHUNK, RET_D), lambda b, c: (b * nc + c, j))
    return pl.pallas_call(
        _ret_kernel,
        out_shape=jax.ShapeDtypeStruct((t, RET_D), BF16),
        grid=(bsz, nc),
        in_specs=[
            col(3), col(4), col(5), col(6),
            pl.BlockSpec((CHUNK, 1), lambda b, c: (b * nc + c, 0)),
            pl.BlockSpec((1, half), lambda b, c: (0, 0)),
            pl.BlockSpec((1, RET_D), lambda b, c: (0, 0)),
        ],
        out_specs=pl.BlockSpec((CHUNK, RET_D), lambda b, c: (b * nc + c, 0)),
        scratch_shapes=[pltpu.VMEM((RET_HEADS, RET_DIM, RET_DIM), F32)],
        compiler_params=_cparams(("parallel", "arbitrary")),
        name="retention",
    )(proj0, proj0, proj0, proj0, pos_col, inv, norm_w.reshape(1, RET_D))


ATT_D = ATT_HEADS * ATT_DIM
IDX_D = IDX_HEADS * IDX_DIM
ATT_ROT = ATT_DIM // ROPE_FRACTION
IDX_ROT = IDX_DIM // ROPE_FRACTION
PREP_TM = 256


def _partial_rope(x, cs, sn, half, period):
    lane = lax.broadcasted_iota(I32, x.shape, 1) % period
    first = lane < half
    second = jnp.logical_and(lane >= half, lane < 2 * half)
    partner = jnp.where(first, pltpu.roll(x, LANES - half, axis=1), pltpu.roll(x, half, axis=1))
    cfac = jnp.where(jnp.logical_or(first, second), cs, 1.0)
    sfac = jnp.where(first, -sn, jnp.where(second, sn, 0.0))
    return x * cfac + partner * sfac


def _dsa_prep_kernel(q_ref, qi_ref, kv_ref, tail_ref, pos_ref, inva_ref, invi_ref, knw_ref,
                     qh_ref, qih_ref, k_ref, vT_ref, ki_ref, wi_ref):
    pos = pos_ref[...].astype(F32)
    ang_a = pos * inva_ref[...]
    cs_a, sn_a = jnp.cos(ang_a), jnp.sin(ang_a)
    ang_i = pos * invi_ref[...]
    cs_i, sn_i = jnp.cos(ang_i), jnp.sin(ang_i)

    for h in range(ATT_HEADS):
        x = q_ref[:, h * ATT_DIM:(h + 1) * ATT_DIM].astype(F32)
        qh_ref[h] = _partial_rope(x, cs_a, sn_a, ATT_ROT // 2, ATT_DIM).astype(BF16)
    for p in range(IDX_HEADS // 2):
        x = qi_ref[:, p * LANES:(p + 1) * LANES].astype(F32)
        y = _partial_rope(x, cs_i, sn_i, IDX_ROT // 2, IDX_DIM).astype(BF16)
        qih_ref[2 * p] = y[:, :IDX_DIM]
        qih_ref[2 * p + 1] = y[:, IDX_DIM:]

    k = kv_ref[:, :ATT_DIM].astype(F32)
    k_ref[...] = _partial_rope(k, cs_a, sn_a, ATT_ROT // 2, ATT_DIM).astype(BF16)
    vT_ref[...] = jnp.transpose(kv_ref[:, ATT_DIM:].astype(F32)).astype(BF16)

    tail = tail_ref[...]
    lane = lax.broadcasted_iota(I32, tail.shape, 1)
    ki = jnp.where(lane < IDX_DIM, tail, 0.0)
    ms = jnp.sum(ki * ki, axis=-1, keepdims=True) * (1.0 / IDX_DIM)
    kin = ki * lax.rsqrt(ms + EPS) * knw_ref[...]
    ki_ref[...] = _partial_rope(kin, cs_i, sn_i, IDX_ROT // 2, IDX_DIM)[:, :IDX_DIM].astype(BF16)
    wi_ref[...] = tail[:, IDX_DIM:IDX_DIM + IDX_HEADS] * (IDX_HEADS ** -0.5 * IDX_DIM ** -0.5)


def _dsa_prep(proj1, tail, pos_col, knw, bsz, seq):
    t = proj1.shape[0]
    tm = PREP_TM
    per_b = seq // tm
    ha, hi = ATT_ROT // 2, IDX_ROT // 2
    inv_a = ROPE_THETA ** (-jnp.arange(ha, dtype=F32) / ha)
    inv_i = ROPE_THETA ** (-jnp.arange(hi, dtype=F32) / hi)
    lane = jnp.arange(LANES)
    inva = jnp.where(lane < ATT_ROT, inv_a[lane % ha], 0.0).reshape(1, LANES)
    invi = jnp.where(lane % IDX_DIM < IDX_ROT, inv_i[lane % hi], 0.0).reshape(1, LANES)
    knw_pad = jnp.zeros((1, LANES), F32).at[0, :IDX_DIM].set(knw)
    return pl.pallas_call(
        _dsa_prep_kernel,
        out_shape=(
            jax.ShapeDtypeStruct((ATT_HEADS, t, ATT_DIM), BF16),
            jax.ShapeDtypeStruct((IDX_HEADS, t, IDX_DIM), BF16),
            jax.ShapeDtypeStruct((t, ATT_DIM), BF16),
            jax.ShapeDtypeStruct((bsz, ATT_DIM, seq), BF16),
            jax.ShapeDtypeStruct((t, IDX_DIM), BF16),
            jax.ShapeDtypeStruct((t, IDX_HEADS), F32),
        ),
        grid=(t // tm,),
        in_specs=[
            pl.BlockSpec((tm, ATT_D), lambda i: (i, 0)),
            pl.BlockSpec((tm, IDX_D), lambda i: (i, ATT_D // IDX_D)),
            pl.BlockSpec((tm, 2 * ATT_DIM), lambda i: (i, (ATT_D + IDX_D) // (2 * ATT_DIM))),
            pl.BlockSpec((tm, LANES), lambda i: (i, 0)),
            pl.BlockSpec((tm, 1), lambda i: (i, 0)),
            pl.BlockSpec((1, LANES), lambda i: (0, 0)),
            pl.BlockSpec((1, LANES), lambda i: (0, 0)),
            pl.BlockSpec((1, LANES), lambda i: (0, 0)),
        ],
        out_specs=(
            pl.BlockSpec((ATT_HEADS, tm, ATT_DIM), lambda i: (0, i, 0)),
            pl.BlockSpec((IDX_HEADS, tm, IDX_DIM), lambda i: (0, i, 0)),
            pl.BlockSpec((tm, ATT_DIM), lambda i: (i, 0)),
            pl.BlockSpec((None, ATT_DIM, tm), lambda i: (i // per_b, 0, i % per_b)),
            pl.BlockSpec((tm, IDX_DIM), lambda i: (i, 0)),
            pl.BlockSpec((tm, IDX_HEADS), lambda i: (i, 0)),
        ),
        compiler_params=_cparams(("parallel",)),
        name="dsa_prep",
    )(proj1, proj1, proj1, tail, pos_col, inva, invi, knw_pad)


DSA_KGROUP = 512
HEAD_PAIR = 2


def _sortable(x):
    x = jnp.where(x == 0.0, 0.0, x)
    bits = pltpu.bitcast(x, I32)
    return jnp.where(bits < 0, bits ^ INT_MAX, bits)


def _dsa_body(lk, topk, j, qh_ref, qih_ref, wiT_ref, k_ref, vT_ref, ki_ref, o_ref):
    qb = Q_BLOCK
    kidx = lax.broadcasted_iota(I32, (lk, qb), 0)
    qpos = j * qb + lax.broadcasted_iota(I32, (lk, qb), 1)
    visible = kidx <= qpos

    ki = ki_ref[0:lk, :]
    score = jnp.zeros((lk, qb), F32)
    for h in range(0, IDX_HEADS, HEAD_PAIR):
        qi = qih_ref[h:h + HEAD_PAIR].reshape(HEAD_PAIR * qb, IDX_DIM)
        s = jnp.maximum(_dot_nt(ki, qi), 0.0)
        for r in range(HEAD_PAIR):
            score = score + s[:, r * qb:(r + 1) * qb] * wiT_ref[h + r:h + r + 1, :]
    keys = _sortable(jnp.where(visible, score, -jnp.inf))

    def count(mask):
        return jnp.sum(mask.astype(I32), axis=0, keepdims=True)

    def bit_step(i, ans):
        trial = jnp.where(i == 0, jnp.zeros_like(ans), ans | jnp.left_shift(1, 31 - i))
        return jnp.where(count(keys >= trial) >= topk, trial, ans)

    thr = lax.fori_loop(0, 32, bit_step, jnp.full((1, qb), INT_MIN, I32))
    above = keys > thr
    equal = keys == thr
    need = topk - count(above)
    over = count(equal) > need

    def tie_cut():
        nbits = max(lk - 1, 1).bit_length()

        def idx_step(i, ans):
            trial = ans | jnp.left_shift(1, nbits - 1 - i)
            c = count(jnp.logical_and(equal, kidx < trial))
            return jnp.where(c < need, trial, ans)

        cut = lax.fori_loop(0, nbits, idx_step, jnp.zeros((1, qb), I32))
        return jnp.where(over, cut, INT_MAX)

    cut = lax.cond(jnp.max(over.astype(I32)) > 0, tie_cut, lambda: jnp.full((1, qb), INT_MAX, I32))
    sel = jnp.logical_and(visible, jnp.logical_or(above, jnp.logical_and(equal, kidx <= cut)))
    bias = jnp.where(sel, 0.0, NEG_BIG)

    k = k_ref[0:lk, :]
    vT = vT_ref[:, 0:lk]
    scale = ATT_DIM ** -0.5
    for h in range(0, ATT_HEADS, HEAD_PAIR):
        qq = qh_ref[h:h + HEAD_PAIR].reshape(HEAD_PAIR * qb, ATT_DIM)
        s = _dot_nt(k, qq) * scale
        s = s + jnp.concatenate([bias] * HEAD_PAIR, axis=1)
        m = jnp.max(s, axis=0, keepdims=True)
        p = jnp.exp(s - m)
        l = jnp.sum(p, axis=0, keepdims=True)
        oT = _dot(vT, p.astype(BF16)) / l
        for r in range(HEAD_PAIR):
            o_ref[:, (h + r) * ATT_DIM:(h + r + 1) * ATT_DIM] = (
                jnp.transpose(oT[:, r * qb:(r + 1) * qb]).astype(o_ref.dtype))


def _dsa_kernel(qh_ref, qih_ref, wiT_ref, k_ref, vT_ref, ki_ref, o_ref, *, seq):
    j = pl.program_id(1)
    per_group = DSA_KGROUP // Q_BLOCK
    topk = min(TOPK, seq // 4)
    for g in range(seq // DSA_KGROUP):
        @pl.when(j // per_group == g)
        def _(g=g):
            _dsa_body((g + 1) * DSA_KGROUP, topk, j, qh_ref, qih_ref, wiT_ref, k_ref, vT_ref, ki_ref, o_ref)


def _dsa_attention(qh, qih, wiT, k_r, vT, ki_r, bsz, seq):
    t = k_r.shape[0]
    nb = seq // Q_BLOCK
    return pl.pallas_call(
        functools.partial(_dsa_kernel, seq=seq),
        out_shape=jax.ShapeDtypeStruct((t, ATT_D), BF16),
        grid=(bsz, nb),
        in_specs=[
            pl.BlockSpec((ATT_HEADS, Q_BLOCK, ATT_DIM), lambda b, j: (0, b * nb + j, 0)),
            pl.BlockSpec((IDX_HEADS, Q_BLOCK, IDX_DIM), lambda b, j: (0, b * nb + j, 0)),
            pl.BlockSpec((IDX_HEADS, Q_BLOCK), lambda b, j: (0, b * nb + j)),
            pl.BlockSpec((seq, ATT_DIM), lambda b, j: (b, 0)),
            pl.BlockSpec((None, ATT_DIM, seq), lambda b, j: (b, 0, 0)),
            pl.BlockSpec((seq, IDX_DIM), lambda b, j: (b, 0)),
        ],
        out_specs=pl.BlockSpec((Q_BLOCK, ATT_D), lambda b, j: (b * nb + j, 0)),
        compiler_params=_cparams(("parallel", "parallel")),
        name="dsa_attention",
    )(qh, qih, wiT, k_r, vT, ki_r)


def _pad_cols(w, n):
    return jnp.pad(w, ((0, 0), (0, n - w.shape[1])))


def _hybrid_mixer(x2, mods_l, norm_w_l, pos_col, w_in, conv_w, conv_b, a_log, dt_bias, d_skip, ssd_norm_w,
                  ret_norm_w, w_out, bsz, seq):
    h = _norm_mod(x2, norm_w_l[2], mods_l, 1, seq)
    n_zx = SSD_D + SSD_D + 2 * SSD_BCW
    dt_lo = n_zx
    ret_lo = dt_lo + SSD_HEADS
    w_main = jnp.concatenate([w_in[:, :n_zx], w_in[:, ret_lo:]], axis=1).astype(BF16)
    proj0 = _matmul(h, w_main, BF16, tm=min(MM_TM, seq), tn=1024)
    w_dt = _pad_cols(w_in[:, dt_lo:ret_lo], LANES)
    dt_pad = _matmul(h, w_dt, F32, tm=min(MM_TM, seq), tn=LANES)
    dtT = jnp.swapaxes(dt_pad[:, :SSD_HEADS].reshape(bsz, seq, SSD_HEADS), 1, 2)
    ya = _ssd(proj0, dt_pad, dtT, conv_w, conv_b, a_log, dt_bias, d_skip, ssd_norm_w, bsz, seq)
    yb = _retention(proj0, pos_col, ret_norm_w, bsz, seq)
    return _proj_resid([ya, yb], w_out, x2, mods_l, 1, norm_w_l[3], seq)


def _dsa_mixer(x2, mods_l, norm_w_l, pos_col, w_in, knw, w_out, bsz, seq):
    h = _norm_mod(x2, norm_w_l[2], mods_l, 1, seq)
    o_k = ATT_D
    o_v = o_k + ATT_DIM
    o_qi = o_v + ATT_DIM
    o_ki = o_qi + IDX_D
    w_main = jnp.concatenate([w_in[:, :o_k], w_in[:, o_qi:o_ki], w_in[:, o_k:o_qi]], axis=1).astype(BF16)
    n_main = w_main.shape[1]
    proj1 = _matmul(h, w_main, BF16, tm=min(MM_TM, seq), tn=n_main // 2)
    tail = _matmul(h, _pad_cols(w_in[:, o_ki:], LANES), F32, tm=min(MM_TM, seq), tn=LANES)
    qh, qih, k_r, vT, ki_r, wi = _dsa_prep(proj1, tail, pos_col, knw, bsz, seq)
    att = _dsa_attention(qh, qih, wi.T, k_r, vT, ki_r, bsz, seq)
    return _proj_resid([att], w_out, x2, mods_l, 1, norm_w_l[3], seq)


def kernel(x, c, positions, mod_w, mod_b, norm_w, ffn_w_gate, ffn_w_up, ffn_w_down, hy_w_in, hy_conv_w,
           hy_conv_b, ssd_A_log, ssd_dt_bias, ssd_D, ssd_norm_w, ret_norm_w, hy_w_out, dsa_w_in,
           idx_k_norm_w, dsa_w_out):
    bsz, seq, d = x.shape
    depth = mod_w.shape[0]
    x2 = x.reshape(bsz * seq, d)
    pos_col = positions.reshape(bsz * seq, 1)
    mods = _mods(c, mod_w, mod_b)
    for i in range(depth):
        ml, nw = mods[i], norm_w[i]
        x2 = _ffn(x2, ml, 0, nw[0], nw[1], ffn_w_gate[i, 0], ffn_w_up[i, 0], ffn_w_down[i, 0], seq)
        if i % 2 == 0:
            e = i // 2
            x2 = _hybrid_mixer(x2, ml, nw, pos_col, hy_w_in[e], hy_conv_w[e], hy_conv_b[e], ssd_A_log[e],
                               ssd_dt_bias[e], ssd_D[e], ssd_norm_w[e], ret_norm_w[e], hy_w_out[e], bsz, seq)
        else:
            o = i // 2
            x2 = _dsa_mixer(x2, ml, nw, pos_col, dsa_w_in[o], idx_k_norm_w[o], dsa_w_out[o], bsz, seq)
        x2 = _ffn(x2, ml, 2, nw[4], nw[5], ffn_w_gate[i, 1], ffn_w_up[i, 1], ffn_w_down[i, 1], seq)
    return x2.reshape(bsz, seq, d)
```

```python
import functools
import math

import jax
import jax.numpy as jnp
from jax import lax
from jax.experimental import pallas as pl
from jax.experimental.pallas import tpu as pltpu

F32 = jnp.float32
BF16 = jnp.bfloat16
I32 = jnp.int32

EPS = 1e-6
SSD_HEADS = 32
SSD_HEAD_DIM = 64
SSD_GROUPS = 8
SSD_STATE = 128
SSD_CONV = 4
CHUNK = 128
RET_HEADS = 8
RET_DIM = 256
RET_THETA = 10000.0
ATT_HEADS = 16
ATT_DIM = 128
IDX_HEADS = 16
IDX_DIM = 64
TOPK = 256
Q_BLOCK = 128
ROPE_THETA = 500000.0
ROPE_FRACTION = 4

V7X_VMEM_BYTES = 64 * 1024 * 1024
VMEM_LIMIT = V7X_VMEM_BYTES - 8 * 1024 * 1024
LANES = 128
SUBLANES = 8

INT_MIN = -(2 ** 31)
INT_MAX = 2 ** 31 - 1
NEG_BIG = -1e30


def _cparams(sem):
    return pltpu.CompilerParams(dimension_semantics=sem, vmem_limit_bytes=VMEM_LIMIT)


def _sigmoid(x):
    return 1.0 / (1.0 + jnp.exp(-x))


def _silu(x):
    return x * _sigmoid(x)


def _softplus(x):
    return jnp.maximum(x, 0.0) + jnp.log1p(jnp.exp(-jnp.abs(x)))


def _rms(x, w):
    return x * lax.rsqrt(jnp.mean(x * x, axis=-1, keepdims=True) + EPS) * w


def _dot(a, b):
    return jnp.dot(a, b, preferred_element_type=F32)


def _dot_nt(a, b):
    return lax.dot_general(a, b, (((1,), (1,)), ((), ())), preferred_element_type=F32)


def _dot_tn(a, b):
    return lax.dot_general(a, b, (((0,), (0,)), ((), ())), preferred_element_type=F32)


def _split3(x):
    hi = x.astype(BF16)
    r = x - hi.astype(F32)
    mid = r.astype(BF16)
    lo = (r - mid.astype(F32)).astype(BF16)
    return hi, mid, lo


def _row_chunks(n_rows, rc, fn):
    def body(i, carry):
        fn(pl.ds(pl.multiple_of(i * rc, rc), rc))
        return carry
    lax.fori_loop(0, n_rows // rc, body, 0)


def _mods_kernel(c_ref, w_ref, b_ref, o_ref):
    cond = _silu(c_ref[...]).astype(BF16)
    o_ref[...] = _dot(cond, w_ref[...].astype(BF16)) + b_ref[...]


def _mods(c, mod_w, mod_b):
    depth, d, n = mod_w.shape
    bsz = c.shape[0]
    c8 = jnp.zeros((SUBLANES, d), F32).at[:bsz].set(c)
    tn = 1024
    out = pl.pallas_call(
        _mods_kernel,
        out_shape=jax.ShapeDtypeStruct((depth, SUBLANES, n), F32),
        grid=(depth, n // tn),
        in_specs=[
            pl.BlockSpec((SUBLANES, d), lambda i, j: (0, 0)),
            pl.BlockSpec((None, d, tn), lambda i, j: (i, 0, j)),
            pl.BlockSpec((None, 1, tn), lambda i, j: (i, 0, j)),
        ],
        out_specs=pl.BlockSpec((None, SUBLANES, tn), lambda i, j: (i, 0, j)),
        compiler_params=_cparams(("parallel", "parallel")),
        name="mods",
    )(c8, mod_w, mod_b.reshape(depth, 1, n))
    return out[:, :bsz].reshape(depth, bsz * 3, 3, d)


def _norm_mod_kernel(x_ref, w_ref, m_ref, o_ref):
    h = _rms(x_ref[...], w_ref[...]) * (1.0 + m_ref[1:2, :]) + m_ref[0:1, :]
    o_ref[...] = h.astype(o_ref.dtype)


def _norm_mod(x2, pre_w, mods_l, sub, seq):
    t, d = x2.shape
    tm = 256
    per_b = seq // tm
    return pl.pallas_call(
        _norm_mod_kernel,
        out_shape=jax.ShapeDtypeStruct((t, d), BF16),
        grid=(t // tm,),
        in_specs=[
            pl.BlockSpec((tm, d), lambda i: (i, 0)),
            pl.BlockSpec((1, d), lambda i: (0, 0)),
            pl.BlockSpec((None, 3, d), lambda i: ((i // per_b) * 3 + sub, 0, 0)),
        ],
        out_specs=pl.BlockSpec((tm, d), lambda i: (i, 0)),
        compiler_params=_cparams(("parallel",)),
        name="norm_mod",
    )(x2, pre_w.reshape(1, d), mods_l)


MM_TM = 1024


def _mm_kernel(a_ref, w_ref, o_ref):
    o_ref[...] = _dot_nt(a_ref[...], w_ref[...].astype(BF16)).astype(o_ref.dtype)


def _matmul_wt(a, w_t, out_dtype, tm, tn, n_out, lead, row0=0):
    m, k = a.shape
    if row0 % tn == 0:
        w_spec = pl.BlockSpec((None, tn, k), lambda i, j: (lead, row0 // tn + j, 0))
    else:
        w_t = w_t[lead]
        assert row0 % SUBLANES == 0 and tn % SUBLANES == 0
        w_spec = pl.BlockSpec((pl.Element(tn), pl.Element(k)),
                              lambda i, j: ((row0 // SUBLANES + j * (tn // SUBLANES)) * SUBLANES, 0))
    return pl.pallas_call(
        _mm_kernel,
        out_shape=jax.ShapeDtypeStruct((m, n_out), out_dtype),
        grid=(m // tm, n_out // tn),
        in_specs=[pl.BlockSpec((tm, k), lambda i, j: (i, 0)), w_spec],
        out_specs=pl.BlockSpec((tm, tn), lambda i, j: (i, j)),
        compiler_params=_cparams(("parallel", "parallel")),
        name="matmul",
    )(a, w_t)


FFN_TM = 1024
FFN_TF = 512
FFN_SUB = 256
ROW_CHUNK = 128


def _ffn_kernel(x_ref, m_ref, prew_ref, postw_ref, wg_ref, wu_ref, wd_ref, o_ref, h_ref, *, nf, tm, tf):
    f = pl.program_id(1)

    @pl.when(f == 0)
    def _():
        def pro(rows):
            h = _rms(x_ref[rows, :], prew_ref[...]) * (1.0 + m_ref[1:2, :]) + m_ref[0:1, :]
            h_ref[rows, :] = h.astype(BF16)
            o_ref[rows, :] = jnp.zeros((ROW_CHUNK, o_ref.shape[1]), F32)
        _row_chunks(tm, ROW_CHUNK, pro)

    h = h_ref[...]
    for c in range(tf // FFN_SUB):
        cs = slice(c * FFN_SUB, (c + 1) * FFN_SUB)
        g = _dot(h, wg_ref[:, cs].astype(BF16))
        u = _dot(h, wu_ref[:, cs].astype(BF16))
        a = (_silu(g) * u).astype(BF16)
        o_ref[...] += _dot(a, wd_ref[cs, :].astype(BF16))

    @pl.when(f == nf - 1)
    def _():
        def epi(rows):
            y = _rms(o_ref[rows, :], postw_ref[...])
            o_ref[rows, :] = x_ref[rows, :] + 0.5 * m_ref[2:3, :] * y
        _row_chunks(tm, ROW_CHUNK, epi)


def _ffn(x2, mods_l, sub, pre_w, post_w, w_gate, w_up, w_down, layer, which, seq):
    t, d = x2.shape
    dff = w_gate.shape[-1]
    tm, tf = min(FFN_TM, seq), FFN_TF
    nf = dff // tf
    per_b = seq // tm
    one = pl.Buffered(1)
    return pl.pallas_call(
        functools.partial(_ffn_kernel, nf=nf, tm=tm, tf=tf),
        out_shape=jax.ShapeDtypeStruct((t, d), F32),
        grid=(t // tm, nf),
        in_specs=[
            pl.BlockSpec((tm, d), lambda i, f: (i, 0), pipeline_mode=one),
            pl.BlockSpec((None, 3, d), lambda i, f: ((i // per_b) * 3 + sub, 0, 0)),
            pl.BlockSpec((1, d), lambda i, f: (0, 0)),
            pl.BlockSpec((1, d), lambda i, f: (0, 0)),
            pl.BlockSpec((None, None, d, tf), lambda i, f: (layer, which, 0, f)),
            pl.BlockSpec((None, None, d, tf), lambda i, f: (layer, which, 0, f)),
            pl.BlockSpec((None, None, tf, d), lambda i, f: (layer, which, f, 0)),
        ],
        out_specs=pl.BlockSpec((tm, d), lambda i, f: (i, 0), pipeline_mode=one),
        scratch_shapes=[pltpu.VMEM((tm, d), BF16)],
        compiler_params=_cparams(("parallel", "arbitrary")),
        name="ffn",
    )(x2, mods_l, pre_w.reshape(1, d), post_w.reshape(1, d), w_gate, w_up, w_down)


PROJ_TM = 1024
PROJ_TK = 1024


def _proj_resid_kernel(*refs, n_src, nk_src, nk, tm):
    a_refs = refs[:n_src]
    w_ref, x_ref, m_ref, postw_ref, o_ref = refs[n_src:]
    k = pl.program_id(1)

    @pl.when(k == 0)
    def _():
        o_ref[...] = jnp.zeros_like(o_ref)

    a = a_refs[0][...]
    for s in range(1, n_src):
        a = jnp.where(k >= s * nk_src, a_refs[s][...], a)
    o_ref[...] += _dot(a, w_ref[...].astype(BF16))

    @pl.when(k == nk - 1)
    def _():
        def epi(rows):
            y = _rms(o_ref[rows, :], postw_ref[...])
            o_ref[rows, :] = x_ref[rows, :] + m_ref[2:3, :] * y
        _row_chunks(tm, ROW_CHUNK, epi)


def _proj_resid(srcs, w, lead, x2, mods_l, sub, post_w, seq):
    t, d = x2.shape
    n_src = len(srcs)
    ksrc = srcs[0].shape[1]
    tm, tk = min(PROJ_TM, seq), PROJ_TK
    nk_src = ksrc // tk
    nk = n_src * nk_src
    per_b = seq // tm
    one = pl.Buffered(1)

    def a_spec(s):
        return pl.BlockSpec((tm, tk), lambda i, k: (i, jnp.clip(k - s * nk_src, 0, nk_src - 1)))

    return pl.pallas_call(
        functools.partial(_proj_resid_kernel, n_src=n_src, nk_src=nk_src, nk=nk, tm=tm),
        out_shape=jax.ShapeDtypeStruct((t, d), F32),
        grid=(t // tm, nk),
        in_specs=[a_spec(s) for s in range(n_src)] + [
            pl.BlockSpec((None, tk, d), lambda i, k: (lead, k, 0)),
            pl.BlockSpec((tm, d), lambda i, k: (i, 0), pipeline_mode=one),
            pl.BlockSpec((None, 3, d), lambda i, k: ((i // per_b) * 3 + sub, 0, 0)),
            pl.BlockSpec((1, d), lambda i, k: (0, 0)),
        ],
        out_specs=pl.BlockSpec((tm, d), lambda i, k: (i, 0), pipeline_mode=one),
        compiler_params=_cparams(("parallel", "arbitrary")),
        name="proj_resid",
    )(*srcs, w, x2, mods_l, post_w.reshape(1, d))


SSD_D = SSD_HEADS * SSD_HEAD_DIM
SSD_GW = SSD_D // SSD_GROUPS
SSD_BCW = SSD_GROUPS * SSD_STATE
TAIL = SUBLANES


def _ssd_kernel(z_ref, xs_ref, bc_ref, dt_ref, dtT_ref, cw_ref, cb_ref, alr_ref, alc_ref, dbr_ref, dbc_ref,
                dexp_ref, nw_ref, e_ref, o_ref, s_ref, ubx_ref, ubc_ref):
    c = pl.program_id(1)
    q = CHUNK

    @pl.when(c == 0)
    def _():
        s_ref[...] = jnp.zeros_like(s_ref)
        ubx_ref[0:TAIL, :] = jnp.zeros((TAIL, SSD_D), F32)
        ubc_ref[0:TAIL, :] = jnp.zeros((TAIL, 2 * SSD_BCW), F32)

    ubx_ref[TAIL:TAIL + q, :] = xs_ref[...].astype(F32)
    ubc_ref[TAIL:TAIL + q, :] = bc_ref[...].astype(F32)

    def conv(ub_ref, lo, hi):
        acc = cb_ref[:, lo:hi]
        for k in range(SSD_CONV):
            off = TAIL - (SSD_CONV - 1) + k
            acc = acc + cw_ref[k:k + 1, lo:hi] * ub_ref[off:off + q, :]
        return _silu(acc)

    xs = conv(ubx_ref, 0, SSD_D)
    bc = conv(ubc_ref, SSD_D, SSD_D + 2 * SSD_BCW)
    ubx_ref[0:TAIL, :] = ubx_ref[q:q + TAIL, :]
    ubc_ref[0:TAIL, :] = ubc_ref[q:q + TAIL, :]

    rows = lax.broadcasted_iota(I32, (q, q), 0)
    cols = lax.broadcasted_iota(I32, (q, q), 1)
    causal = rows >= cols
    tril = causal.astype(BF16)
    triu = (rows <= cols).astype(BF16)

    dt = _softplus(dt_ref[:, 0:SSD_HEADS] + dbr_ref[...])
    dA = dt * (-jnp.exp(alr_ref[...]))
    cum = sum(_dot(tril, p) for p in _split3(dA))
    dtT = _softplus(dtT_ref[...] + dbc_ref[...])
    dAT = dtT * (-jnp.exp(alc_ref[...]))
    cumT = sum(_dot(p, triu) for p in _split3(dAT))

    ecum = jnp.exp(cum)
    decs = jnp.exp(cum[q - 1:q, :] - cum)
    stack = jnp.concatenate([dt, ecum, decs], axis=0)
    ex = sum(_dot(p, e_ref[...]) for p in _split3(stack))
    dt_x, ecum_x, decs_x = ex[0:q], ex[q:2 * q], ex[2 * q:3 * q]
    cdec_x = ecum_x[q - 1:q, :]

    xdt = xs * dt_x
    xw = (xdt * decs_x).astype(BF16)
    xdt_b = xdt.astype(BF16)

    y_parts = []
    for g in range(SSD_GROUPS):
        b_g = bc[:, g * SSD_STATE:(g + 1) * SSD_STATE].astype(BF16)
        c_g = bc[:, SSD_BCW + g * SSD_STATE:SSD_BCW + (g + 1) * SSD_STATE].astype(BF16)
        gs = slice(g * SSD_GW, (g + 1) * SSD_GW)
        cb = _dot_nt(c_g, b_g)
        s_g = s_ref[g]
        y_g = _dot(c_g, s_g.astype(BF16)) * ecum_x[:, gs]
        diag = []
        for r in range(SSD_HEADS // SSD_GROUPS):
            hh = g * (SSD_HEADS // SSD_GROUPS) + r
            seg = cum[:, hh:hh + 1] - cumT[hh:hh + 1, :]
            dec = jnp.where(causal, jnp.exp(seg), 0.0)
            m = (cb * dec).astype(BF16)
            hs = slice(g * SSD_GW + r * SSD_HEAD_DIM, g * SSD_GW + (r + 1) * SSD_HEAD_DIM)
            diag.append(_dot(m, xdt_b[:, hs]))
        y_parts.append(y_g + jnp.concatenate(diag, axis=1))
        s_ref[g] = s_g * cdec_x[:, gs] + _dot_tn(b_g, xw[:, gs])
    y = jnp.concatenate(y_parts, axis=1) + xs * dexp_ref[...]
    o_ref[...] = _rms(y * _silu(z_ref[...].astype(F32)), nw_ref[...]).astype(o_ref.dtype)


def _ssd(proj0, dt_pad, dtT, conv_w, conv_b, a_log, dt_bias, d_skip, norm_w, bsz, seq):
    t = proj0.shape[0]
    nc = seq // CHUNK
    h = SSD_HEADS
    expand = jnp.repeat(jnp.eye(h, dtype=BF16), SSD_HEAD_DIM, axis=1)
    dexp = jnp.repeat(d_skip.astype(F32), SSD_HEAD_DIM).reshape(1, SSD_D)
    full = lambda shape: pl.BlockSpec(shape, lambda b, c: (0,) * len(shape))
    col = lambda j: pl.BlockSpec((CHUNK, SSD_D), lambda b, c: (b * nc + c, j))
    return pl.pallas_call(
        _ssd_kernel,
        out_shape=jax.ShapeDtypeStruct((t, SSD_D), BF16),
        grid=(bsz, nc),
        in_specs=[
            col(0), col(1), col(2),
            pl.BlockSpec((CHUNK, LANES), lambda b, c: (b * nc + c, 0)),
            pl.BlockSpec((None, h, CHUNK), lambda b, c: (b, 0, c)),
            full((SSD_CONV, SSD_D + 2 * SSD_BCW)), full((1, SSD_D + 2 * SSD_BCW)),
            full((1, h)), full((h, 1)), full((1, h)), full((h, 1)),
            full((1, SSD_D)), full((1, SSD_D)), full((h, SSD_D)),
        ],
        out_specs=pl.BlockSpec((CHUNK, SSD_D), lambda b, c: (b * nc + c, 0)),
        scratch_shapes=[
            pltpu.VMEM((SSD_GROUPS, SSD_STATE, SSD_GW), F32),
            pltpu.VMEM((CHUNK + TAIL, SSD_D), F32),
            pltpu.VMEM((CHUNK + TAIL, 2 * SSD_BCW), F32),
        ],
        compiler_params=_cparams(("parallel", "arbitrary")),
        name="ssd",
    )(proj0, proj0, proj0, dt_pad, dtT, conv_w, conv_b.reshape(1, -1),
      a_log.reshape(1, h), a_log.reshape(h, 1), dt_bias.reshape(1, h), dt_bias.reshape(h, 1),
      dexp, norm_w.reshape(1, SSD_D), expand)


RET_D = RET_HEADS * RET_DIM


def _ret_kernel(q_ref, k_ref, v_ref, g_ref, pos_ref, inv_ref, nw_ref, o_ref, r_ref):
    c = pl.program_id(1)
    q = CHUNK
    half = RET_DIM // 2

    @pl.when(c == 0)
    def _():
        r_ref[...] = jnp.zeros_like(r_ref)

    ang = pos_ref[...].astype(F32) * inv_ref[...]
    cs, sn = jnp.cos(ang), jnp.sin(ang)
    li = lax.broadcasted_iota(I32, (q, q), 0)
    si = lax.broadcasted_iota(I32, (q, q), 1)
    dist = (li - si).astype(F32)
    lcol = lax.broadcasted_iota(I32, (q, 1), 0).astype(F32)

    def rope(x):
        x1, x2 = x[:, :half], x[:, half:]
        return jnp.concatenate([x1 * cs - x2 * sn, x1 * sn + x2 * cs], axis=1)

    for h in range(RET_HEADS):
        lg = math.log(1.0 - 2.0 ** (-5.0 - h))
        hs = slice(h * RET_DIM, (h + 1) * RET_DIM)
        qr = rope(q_ref[:, hs].astype(F32)).astype(BF16)
        kr = rope(k_ref[:, hs].astype(F32)) * (RET_DIM ** -0.5)
        v = v_ref[:, hs]
        intra = jnp.where(dist >= 0.0, jnp.exp(dist * lg), 0.0)
        sc = _dot_nt(qr, kr.astype(BF16)) * intra
        r_h = r_ref[h]
        y = _dot(sc.astype(BF16), v) + _dot(qr, r_h.astype(BF16)) * jnp.exp((lcol + 1.0) * lg)
        kd = (kr * jnp.exp((q - 1.0 - lcol) * lg)).astype(BF16)
        r_ref[h] = r_h * math.exp(q * lg) + _dot_tn(kd, v)
        gate = _silu(g_ref[:, hs].astype(F32))
        o_ref[:, hs] = (gate * _rms(y, nw_ref[:, hs])).astype(o_ref.dtype)


def _retention(proj0, pos_col, norm_w, bsz, seq):
    t = proj0.shape[0]
    nc = seq // CHUNK
    half = RET_DIM // 2
    inv = (RET_THETA ** (-jnp.arange(half, dtype=F32) / half)).reshape(1, half)
    col = lambda j: pl.BlockSpec((CHUNK, RET_D), lambda b, c: (b * nc + c, j))
    return pl.pallas_call(
        _ret_kernel,
        out_shape=jax.ShapeDtypeStruct((t, RET_D), BF16),
        grid=(bsz, nc),
        in_specs=[
            col(0), col(1), col(2), col(3),
            pl.BlockSpec((CHUNK, 1), lambda b, c: (b * nc + c, 0)),
            pl.BlockSpec((1, half), lambda b, c: (0, 0)),
            pl.BlockSpec((1, RET_D), lambda b, c: (0, 0)),
        ],
        out_specs=pl.BlockSpec((CHUNK, RET_D), lambda b, c: (b * nc + c, 0)),
        scratch_shapes=[pltpu.VMEM((RET_HEADS, RET_DIM, RET_DIM), F32)],
        compiler_params=_cparams(("parallel", "arbitrary")),
        name="retention",
    )(proj0, proj0, proj0, proj0, pos_col, inv, norm_w.reshape(1, RET_D))


ATT_D = ATT_HEADS * ATT_DIM
IDX_D = IDX_HEADS * IDX_DIM
ATT_ROT = ATT_DIM // ROPE_FRACTION
IDX_ROT = IDX_DIM // ROPE_FRACTION
PREP_TM = 256
Q_SCALE = ATT_DIM ** -0.5 * math.log2(math.e)


def _rope_factors(ang, half, period):
    lane = lax.broadcasted_iota(I32, ang.shape, 1) % period
    first = lane < half
    second = jnp.logical_and(lane >= half, lane < 2 * half)
    cs, sn = jnp.cos(ang), jnp.sin(ang)
    cfac = jnp.where(jnp.logical_or(first, second), cs, 1.0)
    sfac = jnp.where(first, -sn, jnp.where(second, sn, 0.0))
    return first, cfac, sfac, half


def _partial_rope(x, factors):
    first, cfac, sfac, half = factors
    partner = jnp.where(first, pltpu.roll(x, LANES - half, axis=1), pltpu.roll(x, half, axis=1))
    return x * cfac + partner * sfac


def _dsa_prep_kernel(p_ref, tail_ref, pos_ref, inva_ref, invi_ref, knw_ref,
                     qh_ref, qih_ref, k_ref, vT_ref, ki_ref, wi_ref):
    pos = pos_ref[...].astype(F32)
    rope_a = _rope_factors(pos * inva_ref[...], ATT_ROT // 2, ATT_DIM)
    rope_i = _rope_factors(pos * invi_ref[...], IDX_ROT // 2, IDX_DIM)
    first_a, cfac_a, sfac_a, half_a = rope_a
    rope_q = (first_a, cfac_a * Q_SCALE, sfac_a * Q_SCALE, half_a)
    o_k, o_v, o_qi = ATT_D, ATT_D + ATT_DIM, ATT_D + 2 * ATT_DIM

    for h in range(ATT_HEADS):
        x = p_ref[:, h * ATT_DIM:(h + 1) * ATT_DIM].astype(F32)
        qh_ref[h] = _partial_rope(x, rope_q).astype(BF16)
    for p in range(IDX_HEADS // 2):
        x = p_ref[:, o_qi + p * LANES:o_qi + (p + 1) * LANES].astype(F32)
        y = _partial_rope(x, rope_i).astype(BF16)
        qih_ref[2 * p] = y[:, :IDX_DIM]
        qih_ref[2 * p + 1] = y[:, IDX_DIM:]

    k = p_ref[:, o_k:o_v].astype(F32)
    k_ref[...] = _partial_rope(k, rope_a).astype(BF16)
    vT_ref[...] = jnp.transpose(p_ref[:, o_v:o_qi].astype(F32)).astype(BF16)

    tail = tail_ref[...]
    lane = lax.broadcasted_iota(I32, tail.shape, 1)
    ki = jnp.where(lane < IDX_DIM, tail, 0.0)
    ms = jnp.sum(ki * ki, axis=-1, keepdims=True) * (1.0 / IDX_DIM)
    kin = ki * lax.rsqrt(ms + EPS) * knw_ref[...]
    ki_ref[...] = _partial_rope(kin, rope_i)[:, :IDX_DIM].astype(BF16)
    wi_ref[...] = tail[:, IDX_DIM:IDX_DIM + IDX_HEADS] * (IDX_HEADS ** -0.5 * IDX_DIM ** -0.5)


def _dsa_prep(proj1, tail, pos_col, knw, bsz, seq):
    t = proj1.shape[0]
    tm = PREP_TM
    per_b = seq // tm
    ha, hi = ATT_ROT // 2, IDX_ROT // 2
    inv_a = ROPE_THETA ** (-jnp.arange(ha, dtype=F32) / ha)
    inv_i = ROPE_THETA ** (-jnp.arange(hi, dtype=F32) / hi)
    lane = jnp.arange(LANES)
    inva = jnp.where(lane < ATT_ROT, inv_a[lane % ha], 0.0).reshape(1, LANES)
    invi = jnp.where(lane % IDX_DIM < IDX_ROT, inv_i[lane % hi], 0.0).reshape(1, LANES)
    knw_pad = jnp.zeros((1, LANES), F32).at[0, :IDX_DIM].set(knw)
    return pl.pallas_call(
        _dsa_prep_kernel,
        out_shape=(
            jax.ShapeDtypeStruct((ATT_HEADS, t, ATT_DIM), BF16),
            jax.ShapeDtypeStruct((IDX_HEADS, t, IDX_DIM), BF16),
            jax.ShapeDtypeStruct((t, ATT_DIM), BF16),
            jax.ShapeDtypeStruct((bsz, ATT_DIM, seq), BF16),
            jax.ShapeDtypeStruct((t, IDX_DIM), BF16),
            jax.ShapeDtypeStruct((t, IDX_HEADS), F32),
        ),
        grid=(t // tm,),
        in_specs=[
            pl.BlockSpec((tm, proj1.shape[1]), lambda i: (i, 0)),
            pl.BlockSpec((tm, LANES), lambda i: (i, 0)),
            pl.BlockSpec((tm, 1), lambda i: (i, 0)),
            pl.BlockSpec((1, LANES), lambda i: (0, 0)),
            pl.BlockSpec((1, LANES), lambda i: (0, 0)),
            pl.BlockSpec((1, LANES), lambda i: (0, 0)),
        ],
        out_specs=(
            pl.BlockSpec((ATT_HEADS, tm, ATT_DIM), lambda i: (0, i, 0)),
            pl.BlockSpec((IDX_HEADS, tm, IDX_DIM), lambda i: (0, i, 0)),
            pl.BlockSpec((tm, ATT_DIM), lambda i: (i, 0)),
            pl.BlockSpec((None, ATT_DIM, tm), lambda i: (i // per_b, 0, i % per_b)),
            pl.BlockSpec((tm, IDX_DIM), lambda i: (i, 0)),
            pl.BlockSpec((tm, IDX_HEADS), lambda i: (i, 0)),
        ),
        compiler_params=_cparams(("parallel",)),
        name="dsa_prep",
    )(proj1, tail, pos_col, inva, invi, knw_pad)


DSA_KGROUP = 256
HEAD_PAIR = 2


REDUCE_SLABS = 8


def _col_reduce(x, reduce_fn, combine_fn):
    n = x.shape[0] // REDUCE_SLABS
    parts = [reduce_fn(x[g * n:(g + 1) * n], axis=0, keepdims=True) for g in range(REDUCE_SLABS)]
    while len(parts) > 1:
        parts = [combine_fn(parts[i], parts[i + 1]) for i in range(0, len(parts), 2)]
    return parts[0]


def _col_sum(x):
    return _col_reduce(x, jnp.sum, jnp.add)


def _col_max(x):
    return _col_reduce(x, jnp.max, jnp.maximum)


def _key_to_float(key):
    return pltpu.bitcast(jnp.where(key < 0, key ^ INT_MAX, key), F32)


def _dsa_body(lk, topk, j, qh_ref, qih_ref, wiT_ref, k_ref, vT_ref, ki_ref, o_ref):
    qb = Q_BLOCK
    kidx = lax.broadcasted_iota(I32, (lk, qb), 0)
    qpos = j * qb + lax.broadcasted_iota(I32, (lk, qb), 1)
    visible = kidx <= qpos

    ki = ki_ref[0:lk, :]
    score = jnp.zeros((lk, qb), F32)
    for h in range(0, IDX_HEADS, HEAD_PAIR):
        qi = qih_ref[h:h + HEAD_PAIR].reshape(HEAD_PAIR * qb, IDX_DIM)
        s = jnp.maximum(_dot_nt(ki, qi), 0.0)
        for r in range(HEAD_PAIR):
            score = score + s[:, r * qb:(r + 1) * qb] * wiT_ref[h + r:h + r + 1, :]
    score = jnp.where(visible, score, -jnp.inf)

    def count(mask):
        return _col_sum(jnp.where(mask, 1, 0).astype(I32))

    def bit_step(i, ans):
        trial = jnp.where(i == 0, jnp.zeros_like(ans), ans | jnp.left_shift(1, 31 - i))
        return jnp.where(count(score >= _key_to_float(trial)) >= topk, trial, ans)

    thr = _key_to_float(lax.fori_loop(0, 32, bit_step, jnp.full((1, qb), INT_MIN, I32)))
    above = score > thr
    equal = score == thr
    need = topk - count(above)
    over = count(equal) > need
    few = (j * qb + lax.broadcasted_iota(I32, (1, qb), 1)) < topk

    def tie_cut():
        nbits = max(lk - 1, 1).bit_length()

        def idx_step(i, ans):
            trial = ans | jnp.left_shift(1, nbits - 1 - i)
            c = count(jnp.logical_and(equal, kidx < trial))
            return jnp.where(c < need, trial, ans)

        cut = lax.fori_loop(0, nbits, idx_step, jnp.zeros((1, qb), I32))
        return jnp.where(over, cut, INT_MAX)

    cut = lax.cond(jnp.max(over.astype(I32)) > 0, tie_cut, lambda: jnp.full((1, qb), INT_MAX, I32))
    picked = jnp.logical_or(above, jnp.logical_and(equal, kidx <= cut))
    sel = jnp.logical_and(visible, jnp.logical_or(few, picked))
    bias = jnp.where(sel, 0.0, NEG_BIG)

    k = k_ref[0:lk, :]
    vT = vT_ref[:, 0:lk]
    bias2 = jnp.concatenate([bias] * HEAD_PAIR, axis=1)
    for h in range(0, ATT_HEADS, HEAD_PAIR):
        qq = qh_ref[h:h + HEAD_PAIR].reshape(HEAD_PAIR * qb, ATT_DIM)
        s = _dot_nt(k, qq) + bias2
        m = _col_max(s)
        p = jnp.exp2(s - m)
        l = _col_sum(p)
        oT = _dot(vT, p.astype(BF16)) / l
        for r in range(HEAD_PAIR):
            o_ref[:, (h + r) * ATT_DIM:(h + r + 1) * ATT_DIM] = (
                jnp.transpose(oT[:, r * qb:(r + 1) * qb]).astype(o_ref.dtype))


def _dsa_kernel(qh_ref, qih_ref, wiT_ref, k_ref, vT_ref, ki_ref, o_ref, *, seq):
    j = pl.program_id(1)
    per_group = DSA_KGROUP // Q_BLOCK
    topk = min(TOPK, seq // 4)
    for g in range(seq // DSA_KGROUP):
        @pl.when(j // per_group == g)
        def _(g=g):
            _dsa_body((g + 1) * DSA_KGROUP, topk, j, qh_ref, qih_ref, wiT_ref, k_ref, vT_ref, ki_ref, o_ref)


def _dsa_attention(qh, qih, wiT, k_r, vT, ki_r, bsz, seq):
    t = k_r.shape[0]
    nb = seq // Q_BLOCK
    return pl.pallas_call(
        functools.partial(_dsa_kernel, seq=seq),
        out_shape=jax.ShapeDtypeStruct((t, ATT_D), BF16),
        grid=(bsz, nb),
        in_specs=[
            pl.BlockSpec((ATT_HEADS, Q_BLOCK, ATT_DIM), lambda b, j: (0, b * nb + j, 0)),
            pl.BlockSpec((IDX_HEADS, Q_BLOCK, IDX_DIM), lambda b, j: (0, b * nb + j, 0)),
            pl.BlockSpec((IDX_HEADS, Q_BLOCK), lambda b, j: (0, b * nb + j)),
            pl.BlockSpec((seq, ATT_DIM), lambda b, j: (b, 0)),
            pl.BlockSpec((None, ATT_DIM, seq), lambda b, j: (b, 0, 0)),
            pl.BlockSpec((seq, IDX_DIM), lambda b, j: (b, 0)),
        ],
        out_specs=pl.BlockSpec((Q_BLOCK, ATT_D), lambda b, j: (b * nb + j, 0)),
        compiler_params=_cparams(("parallel", "parallel")),
        name="dsa_attention",
    )(qh, qih, wiT, k_r, vT, ki_r)


def _pad_rows(w, n):
    return jnp.pad(w, ((0, n - w.shape[0]), (0, 0)))


def _hybrid_mixer(x2, mods_l, norm_w_l, pos_col, w_in, e, conv_w, conv_b, a_log, dt_bias, d_skip, ssd_norm_w,
                  ret_norm_w, w_out, bsz, seq):
    h = _norm_mod(x2, norm_w_l[2], mods_l, 1, seq)
    tm = min(MM_TM, seq)
    n_zx = SSD_D + SSD_D + 2 * SSD_BCW
    ret_lo = n_zx + SSD_HEADS
    w_t = jnp.swapaxes(w_in, 1, 2)
    proj_zx = _matmul_wt(h, w_t, BF16, tm, 1024, n_zx, e)
    proj_ret = _matmul_wt(h, w_t, BF16, tm, 1024, w_t.shape[1] - ret_lo, e, row0=ret_lo)
    w_dt = _pad_rows(w_t[e, n_zx:ret_lo], LANES)[None]
    dt_pad = _matmul_wt(h, w_dt, F32, tm, LANES, LANES, 0)
    dtT = jnp.swapaxes(dt_pad[:, :SSD_HEADS].reshape(bsz, seq, SSD_HEADS), 1, 2)
    ya = _ssd(proj_zx, dt_pad, dtT, conv_w, conv_b, a_log, dt_bias, d_skip, ssd_norm_w, bsz, seq)
    yb = _retention(proj_ret, pos_col, ret_norm_w, bsz, seq)
    return _proj_resid([ya, yb], w_out, e, x2, mods_l, 1, norm_w_l[3], seq)


def _dsa_mixer(x2, mods_l, norm_w_l, pos_col, w_in, o, knw, w_out, bsz, seq):
    h = _norm_mod(x2, norm_w_l[2], mods_l, 1, seq)
    tm = min(MM_TM, seq)
    n_main = ATT_D + 2 * ATT_DIM + IDX_D
    w_t = jnp.swapaxes(w_in, 1, 2)
    proj1 = _matmul_wt(h, w_t, BF16, tm, n_main // 2, n_main, o)
    tail = _matmul_wt(h, _pad_rows(w_t[o, n_main:], LANES)[None], F32, tm, LANES, LANES, 0)
    qh, qih, k_r, vT, ki_r, wi = _dsa_prep(proj1, tail, pos_col, knw, bsz, seq)
    att = _dsa_attention(qh, qih, wi.T, k_r, vT, ki_r, bsz, seq)
    return _proj_resid([att], w_out, o, x2, mods_l, 1, norm_w_l[3], seq)


def kernel(x, c, positions, mod_w, mod_b, norm_w, ffn_w_gate, ffn_w_up, ffn_w_down, hy_w_in, hy_conv_w,
           hy_conv_b, ssd_A_log, ssd_dt_bias, ssd_D, ssd_norm_w, ret_norm_w, hy_w_out, dsa_w_in,
           idx_k_norm_w, dsa_w_out):
    bsz, seq, d = x.shape
    depth = mod_w.shape[0]
    x2 = x.reshape(bsz * seq, d)
    pos_col = positions.reshape(bsz * seq, 1)
    mods = _mods(c, mod_w, mod_b)
    for i in range(depth):
        ml, nw = mods[i], norm_w[i]
        x2 = _ffn(x2, ml, 0, nw[0], nw[1], ffn_w_gate, ffn_w_up, ffn_w_down, i, 0, seq)
        if i % 2 == 0:
            e = i // 2
            x2 = _hybrid_mixer(x2, ml, nw, pos_col, hy_w_in, e, hy_conv_w[e], hy_conv_b[e], ssd_A_log[e],
                               ssd_dt_bias[e], ssd_D[e], ssd_norm_w[e], ret_norm_w[e], hy_w_out, bsz, seq)
        else:
            o = i // 2
            x2 = _dsa_mixer(x2, ml, nw, pos_col, dsa_w_in, o, idx_k_norm_w[o], dsa_w_out, bsz, seq)
        x2 = _ffn(x2, ml, 2, nw[4], nw[5], ffn_w_gate, ffn_w_up, ffn_w_down, i, 1, seq)
    return x2.reshape(bsz, seq, d)
```

```python
import functools
import math

import jax
import jax.numpy as jnp
from jax import lax
from jax.experimental import pallas as pl
from jax.experimental.pallas import tpu as pltpu

F32 = jnp.float32
BF16 = jnp.bfloat16
I32 = jnp.int32

EPS = 1e-6
SSD_HEADS = 32
SSD_HEAD_DIM = 64
SSD_GROUPS = 8
SSD_STATE = 128
SSD_CONV = 4
CHUNK = 128
RET_HEADS = 8
RET_DIM = 256
RET_THETA = 10000.0
ATT_HEADS = 16
ATT_DIM = 128
IDX_HEADS = 16
IDX_DIM = 64
TOPK = 256
Q_BLOCK = 128
ROPE_THETA = 500000.0
ROPE_FRACTION = 4

V7X_VMEM_BYTES = 64 * 1024 * 1024
VMEM_LIMIT = V7X_VMEM_BYTES - 8 * 1024 * 1024
LANES = 128
SUBLANES = 8

INT_MIN = -(2 ** 31)
INT_MAX = 2 ** 31 - 1
NEG_BIG = -1e30


def _cparams(sem):
    return pltpu.CompilerParams(dimension_semantics=sem, vmem_limit_bytes=VMEM_LIMIT)


def _sigmoid(x):
    return 1.0 / (1.0 + jnp.exp(-x))


def _silu(x):
    return x * _sigmoid(x)


def _softplus(x):
    return jnp.maximum(x, 0.0) + jnp.log1p(jnp.exp(-jnp.abs(x)))


def _rms(x, w):
    return x * lax.rsqrt(jnp.mean(x * x, axis=-1, keepdims=True) + EPS) * w


def _dot(a, b):
    return jnp.dot(a, b, preferred_element_type=F32)


def _dot_nt(a, b):
    return lax.dot_general(a, b, (((1,), (1,)), ((), ())), preferred_element_type=F32)


def _dot_tn(a, b):
    return lax.dot_general(a, b, (((0,), (0,)), ((), ())), preferred_element_type=F32)


def _split3(x):
    hi = x.astype(BF16)
    r = x - hi.astype(F32)
    mid = r.astype(BF16)
    lo = (r - mid.astype(F32)).astype(BF16)
    return hi, mid, lo


def _row_chunks(n_rows, rc, fn):
    def body(i, carry):
        fn(pl.ds(pl.multiple_of(i * rc, rc), rc))
        return carry
    lax.fori_loop(0, n_rows // rc, body, 0)


def _mods_kernel(c_ref, w_ref, b_ref, o_ref):
    cond = _silu(c_ref[...]).astype(BF16)
    o_ref[...] = _dot(cond, w_ref[...].astype(BF16)) + b_ref[...]


def _mods(c, mod_w, mod_b):
    depth, d, n = mod_w.shape
    bsz = c.shape[0]
    c8 = jnp.zeros((SUBLANES, d), F32).at[:bsz].set(c)
    tn = 1024
    out = pl.pallas_call(
        _mods_kernel,
        out_shape=jax.ShapeDtypeStruct((depth, SUBLANES, n), F32),
        grid=(depth, n // tn),
        in_specs=[
            pl.BlockSpec((SUBLANES, d), lambda i, j: (0, 0)),
            pl.BlockSpec((None, d, tn), lambda i, j: (i, 0, j)),
            pl.BlockSpec((None, 1, tn), lambda i, j: (i, 0, j)),
        ],
        out_specs=pl.BlockSpec((None, SUBLANES, tn), lambda i, j: (i, 0, j)),
        compiler_params=_cparams(("parallel", "parallel")),
        name="mods",
    )(c8, mod_w, mod_b.reshape(depth, 1, n))
    return out[:, :bsz].reshape(depth, bsz * 3, 3, d)


def _norm_mod_kernel(x_ref, w_ref, m_ref, o_ref):
    h = _rms(x_ref[...], w_ref[...]) * (1.0 + m_ref[1:2, :]) + m_ref[0:1, :]
    o_ref[...] = h.astype(o_ref.dtype)


def _norm_mod(x2, pre_w, mods_l, sub, seq):
    t, d = x2.shape
    tm = 256
    per_b = seq // tm
    return pl.pallas_call(
        _norm_mod_kernel,
        out_shape=jax.ShapeDtypeStruct((t, d), BF16),
        grid=(t // tm,),
        in_specs=[
            pl.BlockSpec((tm, d), lambda i: (i, 0)),
            pl.BlockSpec((1, d), lambda i: (0, 0)),
            pl.BlockSpec((None, 3, d), lambda i: ((i // per_b) * 3 + sub, 0, 0)),
        ],
        out_specs=pl.BlockSpec((tm, d), lambda i: (i, 0)),
        compiler_params=_cparams(("parallel",)),
        name="norm_mod",
    )(x2, pre_w.reshape(1, d), mods_l)


MM_TM = 1024


def _mm_kernel(a_ref, w_ref, o_ref):
    o_ref[...] = _dot_nt(a_ref[...], w_ref[...].astype(BF16)).astype(o_ref.dtype)


def _matmul_wt(a, w_t, out_dtype, tm, tn, n_out, lead, row0=0):
    m, k = a.shape
    if row0 % tn == 0:
        w_spec = pl.BlockSpec((None, tn, k), lambda i, j: (lead, row0 // tn + j, 0))
    else:
        w_t = w_t[lead]
        assert row0 % SUBLANES == 0 and tn % SUBLANES == 0
        w_spec = pl.BlockSpec((pl.Element(tn), pl.Element(k)),
                              lambda i, j: ((row0 // SUBLANES + j * (tn // SUBLANES)) * SUBLANES, 0))
    return pl.pallas_call(
        _mm_kernel,
        out_shape=jax.ShapeDtypeStruct((m, n_out), out_dtype),
        grid=(m // tm, n_out // tn),
        in_specs=[pl.BlockSpec((tm, k), lambda i, j: (i, 0)), w_spec],
        out_specs=pl.BlockSpec((tm, tn), lambda i, j: (i, j)),
        compiler_params=_cparams(("parallel", "parallel")),
        name="matmul",
    )(a, w_t)


FFN_TM = 1024
FFN_TF = 512
FFN_SUB = 256
ROW_CHUNK = 128


def _ffn_kernel(x_ref, m_ref, prew_ref, postw_ref, wg_ref, wu_ref, wd_ref, o_ref, h_ref, *, nf, tm, tf):
    f = pl.program_id(1)

    @pl.when(f == 0)
    def _():
        def pro(rows):
            h = _rms(x_ref[rows, :], prew_ref[...]) * (1.0 + m_ref[1:2, :]) + m_ref[0:1, :]
            h_ref[rows, :] = h.astype(BF16)
            o_ref[rows, :] = jnp.zeros((ROW_CHUNK, o_ref.shape[1]), F32)
        _row_chunks(tm, ROW_CHUNK, pro)

    h = h_ref[...]
    for c in range(tf // FFN_SUB):
        cs = slice(c * FFN_SUB, (c + 1) * FFN_SUB)
        g = _dot(h, wg_ref[:, cs].astype(BF16))
        u = _dot(h, wu_ref[:, cs].astype(BF16))
        a = (_silu(g) * u).astype(BF16)
        o_ref[...] += _dot(a, wd_ref[cs, :].astype(BF16))

    @pl.when(f == nf - 1)
    def _():
        def epi(rows):
            y = _rms(o_ref[rows, :], postw_ref[...])
            o_ref[rows, :] = x_ref[rows, :] + 0.5 * m_ref[2:3, :] * y
        _row_chunks(tm, ROW_CHUNK, epi)


def _ffn(x2, mods_l, sub, pre_w, post_w, w_gate, w_up, w_down, layer, which, seq):
    t, d = x2.shape
    dff = w_gate.shape[-1]
    tm, tf = min(FFN_TM, seq), FFN_TF
    nf = dff // tf
    per_b = seq // tm
    one = pl.Buffered(1)
    return pl.pallas_call(
        functools.partial(_ffn_kernel, nf=nf, tm=tm, tf=tf),
        out_shape=jax.ShapeDtypeStruct((t, d), F32),
        grid=(t // tm, nf),
        in_specs=[
            pl.BlockSpec((tm, d), lambda i, f: (i, 0), pipeline_mode=one),
            pl.BlockSpec((None, 3, d), lambda i, f: ((i // per_b) * 3 + sub, 0, 0)),
            pl.BlockSpec((1, d), lambda i, f: (0, 0)),
            pl.BlockSpec((1, d), lambda i, f: (0, 0)),
            pl.BlockSpec((None, None, d, tf), lambda i, f: (layer, which, 0, f)),
            pl.BlockSpec((None, None, d, tf), lambda i, f: (layer, which, 0, f)),
            pl.BlockSpec((None, None, tf, d), lambda i, f: (layer, which, f, 0)),
        ],
        out_specs=pl.BlockSpec((tm, d), lambda i, f: (i, 0), pipeline_mode=one),
        scratch_shapes=[pltpu.VMEM((tm, d), BF16)],
        compiler_params=_cparams(("parallel", "arbitrary")),
        name="ffn",
    )(x2, mods_l, pre_w.reshape(1, d), post_w.reshape(1, d), w_gate, w_up, w_down)


PROJ_TM = 1024
PROJ_TK = 1024


def _proj_resid_kernel(*refs, n_src, nk_src, nk, tm):
    a_refs = refs[:n_src]
    w_ref, x_ref, m_ref, postw_ref, o_ref = refs[n_src:]
    k = pl.program_id(1)

    @pl.when(k == 0)
    def _():
        o_ref[...] = jnp.zeros_like(o_ref)

    a = a_refs[0][...]
    for s in range(1, n_src):
        a = jnp.where(k >= s * nk_src, a_refs[s][...], a)
    o_ref[...] += _dot(a, w_ref[...].astype(BF16))

    @pl.when(k == nk - 1)
    def _():
        def epi(rows):
            y = _rms(o_ref[rows, :], postw_ref[...])
            o_ref[rows, :] = x_ref[rows, :] + m_ref[2:3, :] * y
        _row_chunks(tm, ROW_CHUNK, epi)


def _proj_resid(srcs, w, lead, x2, mods_l, sub, post_w, seq):
    t, d = x2.shape
    n_src = len(srcs)
    ksrc = srcs[0].shape[1]
    tm, tk = min(PROJ_TM, seq), PROJ_TK
    nk_src = ksrc // tk
    nk = n_src * nk_src
    per_b = seq // tm
    one = pl.Buffered(1)

    def a_spec(s):
        return pl.BlockSpec((tm, tk), lambda i, k: (i, jnp.clip(k - s * nk_src, 0, nk_src - 1)))

    return pl.pallas_call(
        functools.partial(_proj_resid_kernel, n_src=n_src, nk_src=nk_src, nk=nk, tm=tm),
        out_shape=jax.ShapeDtypeStruct((t, d), F32),
        grid=(t // tm, nk),
        in_specs=[a_spec(s) for s in range(n_src)] + [
            pl.BlockSpec((None, tk, d), lambda i, k: (lead, k, 0)),
            pl.BlockSpec((tm, d), lambda i, k: (i, 0), pipeline_mode=one),
            pl.BlockSpec((None, 3, d), lambda i, k: ((i // per_b) * 3 + sub, 0, 0)),
            pl.BlockSpec((1, d), lambda i, k: (0, 0)),
        ],
        out_specs=pl.BlockSpec((tm, d), lambda i, k: (i, 0), pipeline_mode=one),
        compiler_params=_cparams(("parallel", "arbitrary")),
        name="proj_resid",
    )(*srcs, w, x2, mods_l, post_w.reshape(1, d))


SSD_D = SSD_HEADS * SSD_HEAD_DIM
SSD_GW = SSD_D // SSD_GROUPS
SSD_BCW = SSD_GROUPS * SSD_STATE
TAIL = SUBLANES


def _ssd_kernel(z_ref, xs_ref, bc_ref, dt_ref, dtT_ref, cw_ref, cb_ref, alr_ref, alc_ref, dbr_ref, dbc_ref,
                dexp_ref, nw_ref, e_ref, o_ref, s_ref, ubx_ref, ubc_ref):
    c = pl.program_id(1)
    q = CHUNK

    @pl.when(c == 0)
    def _():
        s_ref[...] = jnp.zeros_like(s_ref)
        ubx_ref[0:TAIL, :] = jnp.zeros((TAIL, SSD_D), F32)
        ubc_ref[0:TAIL, :] = jnp.zeros((TAIL, 2 * SSD_BCW), F32)

    ubx_ref[TAIL:TAIL + q, :] = xs_ref[...].astype(F32)
    ubc_ref[TAIL:TAIL + q, :] = bc_ref[...].astype(F32)

    def conv(ub_ref, lo, hi):
        acc = cb_ref[:, lo:hi]
        for k in range(SSD_CONV):
            off = TAIL - (SSD_CONV - 1) + k
            acc = acc + cw_ref[k:k + 1, lo:hi] * ub_ref[off:off + q, :]
        return _silu(acc)

    xs = conv(ubx_ref, 0, SSD_D)
    bc = conv(ubc_ref, SSD_D, SSD_D + 2 * SSD_BCW)
    ubx_ref[0:TAIL, :] = ubx_ref[q:q + TAIL, :]
    ubc_ref[0:TAIL, :] = ubc_ref[q:q + TAIL, :]

    rows = lax.broadcasted_iota(I32, (q, q), 0)
    cols = lax.broadcasted_iota(I32, (q, q), 1)
    causal = rows >= cols
    tril = causal.astype(BF16)
    triu = (rows <= cols).astype(BF16)

    dt = _softplus(dt_ref[:, 0:SSD_HEADS] + dbr_ref[...])
    dA = dt * (-jnp.exp(alr_ref[...]))
    cum = sum(_dot(tril, p) for p in _split3(dA))
    dtT = _softplus(dtT_ref[...] + dbc_ref[...])
    dAT = dtT * (-jnp.exp(alc_ref[...]))
    cumT = sum(_dot(p, triu) for p in _split3(dAT))

    ecum = jnp.exp(cum)
    decs = jnp.exp(cum[q - 1:q, :] - cum)
    stack = jnp.concatenate([dt, ecum, decs], axis=0)
    ex = sum(_dot(p, e_ref[...]) for p in _split3(stack))
    dt_x, ecum_x, decs_x = ex[0:q], ex[q:2 * q], ex[2 * q:3 * q]
    cdec_x = ecum_x[q - 1:q, :]

    xdt = xs * dt_x
    xw = (xdt * decs_x).astype(BF16)
    xdt_b = xdt.astype(BF16)

    y_parts = []
    for g in range(SSD_GROUPS):
        b_g = bc[:, g * SSD_STATE:(g + 1) * SSD_STATE].astype(BF16)
        c_g = bc[:, SSD_BCW + g * SSD_STATE:SSD_BCW + (g + 1) * SSD_STATE].astype(BF16)
        gs = slice(g * SSD_GW, (g + 1) * SSD_GW)
        cb = _dot_nt(c_g, b_g)
        s_g = s_ref[g]
        y_g = _dot(c_g, s_g.astype(BF16)) * ecum_x[:, gs]
        diag = []
        for r in range(SSD_HEADS // SSD_GROUPS):
            hh = g * (SSD_HEADS // SSD_GROUPS) + r
            seg = cum[:, hh:hh + 1] - cumT[hh:hh + 1, :]
            dec = jnp.where(causal, jnp.exp(seg), 0.0)
            m = (cb * dec).astype(BF16)
            hs = slice(g * SSD_GW + r * SSD_HEAD_DIM, g * SSD_GW + (r + 1) * SSD_HEAD_DIM)
            diag.append(_dot(m, xdt_b[:, hs]))
        y_parts.append(y_g + jnp.concatenate(diag, axis=1))
        s_ref[g] = s_g * cdec_x[:, gs] + _dot_tn(b_g, xw[:, gs])
    y = jnp.concatenate(y_parts, axis=1) + xs * dexp_ref[...]
    o_ref[...] = _rms(y * _silu(z_ref[...].astype(F32)), nw_ref[...]).astype(o_ref.dtype)


def _ssd(proj0, dt_pad, dtT, conv_w, conv_b, a_log, dt_bias, d_skip, norm_w, bsz, seq):
    t = proj0.shape[0]
    nc = seq // CHUNK
    h = SSD_HEADS
    expand = jnp.repeat(jnp.eye(h, dtype=BF16), SSD_HEAD_DIM, axis=1)
    dexp = jnp.repeat(d_skip.astype(F32), SSD_HEAD_DIM).reshape(1, SSD_D)
    full = lambda shape: pl.BlockSpec(shape, lambda b, c: (0,) * len(shape))
    col = lambda j: pl.BlockSpec((CHUNK, SSD_D), lambda b, c: (b * nc + c, j))
    return pl.pallas_call(
        _ssd_kernel,
        out_shape=jax.ShapeDtypeStruct((t, SSD_D), BF16),
        grid=(bsz, nc),
        in_specs=[
            col(0), col(1), col(2),
            pl.BlockSpec((CHUNK, LANES), lambda b, c: (b * nc + c, 0)),
            pl.BlockSpec((None, h, CHUNK), lambda b, c: (b, 0, c)),
            full((SSD_CONV, SSD_D + 2 * SSD_BCW)), full((1, SSD_D + 2 * SSD_BCW)),
            full((1, h)), full((h, 1)), full((1, h)), full((h, 1)),
            full((1, SSD_D)), full((1, SSD_D)), full((h, SSD_D)),
        ],
        out_specs=pl.BlockSpec((CHUNK, SSD_D), lambda b, c: (b * nc + c, 0)),
        scratch_shapes=[
            pltpu.VMEM((SSD_GROUPS, SSD_STATE, SSD_GW), F32),
            pltpu.VMEM((CHUNK + TAIL, SSD_D), F32),
            pltpu.VMEM((CHUNK + TAIL, 2 * SSD_BCW), F32),
        ],
        compiler_params=_cparams(("parallel", "arbitrary")),
        name="ssd",
    )(proj0, proj0, proj0, dt_pad, dtT, conv_w, conv_b.reshape(1, -1),
      a_log.reshape(1, h), a_log.reshape(h, 1), dt_bias.reshape(1, h), dt_bias.reshape(h, 1),
      dexp, norm_w.reshape(1, SSD_D), expand)


RET_D = RET_HEADS * RET_DIM


def _ret_kernel(q_ref, k_ref, v_ref, g_ref, pos_ref, inv_ref, nw_ref, o_ref, r_ref):
    c = pl.program_id(1)
    q = CHUNK
    half = RET_DIM // 2

    @pl.when(c == 0)
    def _():
        r_ref[...] = jnp.zeros_like(r_ref)

    ang = pos_ref[...].astype(F32) * inv_ref[...]
    cs, sn = jnp.cos(ang), jnp.sin(ang)
    li = lax.broadcasted_iota(I32, (q, q), 0)
    si = lax.broadcasted_iota(I32, (q, q), 1)
    dist = (li - si).astype(F32)
    lcol = lax.broadcasted_iota(I32, (q, 1), 0).astype(F32)

    def rope(x):
        x1, x2 = x[:, :half], x[:, half:]
        return jnp.concatenate([x1 * cs - x2 * sn, x1 * sn + x2 * cs], axis=1)

    for h in range(RET_HEADS):
        lg = math.log(1.0 - 2.0 ** (-5.0 - h))
        hs = slice(h * RET_DIM, (h + 1) * RET_DIM)
        qr = rope(q_ref[:, hs].astype(F32)).astype(BF16)
        kr = rope(k_ref[:, hs].astype(F32)) * (RET_DIM ** -0.5)
        v = v_ref[:, hs]
        intra = jnp.where(dist >= 0.0, jnp.exp(dist * lg), 0.0)
        sc = _dot_nt(qr, kr.astype(BF16)) * intra
        r_h = r_ref[h]
        y = _dot(sc.astype(BF16), v) + _dot(qr, r_h.astype(BF16)) * jnp.exp((lcol + 1.0) * lg)
        kd = (kr * jnp.exp((q - 1.0 - lcol) * lg)).astype(BF16)
        r_ref[h] = r_h * math.exp(q * lg) + _dot_tn(kd, v)
        gate = _silu(g_ref[:, hs].astype(F32))
        o_ref[:, hs] = (gate * _rms(y, nw_ref[:, hs])).astype(o_ref.dtype)


def _retention(proj0, pos_col, norm_w, bsz, seq):
    t = proj0.shape[0]
    nc = seq // CHUNK
    half = RET_DIM // 2
    inv = (RET_THETA ** (-jnp.arange(half, dtype=F32) / half)).reshape(1, half)
    col = lambda j: pl.BlockSpec((CHUNK, RET_D), lambda b, c: (b * nc + c, j))
    return pl.pallas_call(
        _ret_kernel,
        out_shape=jax.ShapeDtypeStruct((t, RET_D), BF16),
        grid=(bsz, nc),
        in_specs=[
            col(0), col(1), col(2), col(3),
            pl.BlockSpec((CHUNK, 1), lambda b, c: (b * nc + c, 0)),
            pl.BlockSpec((1, half), lambda b, c: (0, 0)),
            pl.BlockSpec((1, RET_D), lambda b, c: (0, 0)),
        ],
        out_specs=pl.BlockSpec((CHUNK, RET_D), lambda b, c: (b * nc + c, 0)),
        scratch_shapes=[pltpu.VMEM((RET_HEADS, RET_DIM, RET_DIM), F32)],
        compiler_params=_cparams(("parallel", "arbitrary")),
        name="retention",
    )(proj0, proj0, proj0, proj0, pos_col, inv, norm_w.reshape(1, RET_D))


ATT_D = ATT_HEADS * ATT_DIM
IDX_D = IDX_HEADS * IDX_DIM
ATT_ROT = ATT_DIM // ROPE_FRACTION
IDX_ROT = IDX_DIM // ROPE_FRACTION
PREP_TM = 256
Q_SCALE = ATT_DIM ** -0.5 * math.log2(math.e)


def _rope_factors(ang, half, period):
    lane = lax.broadcasted_iota(I32, ang.shape, 1) % period
    first = lane < half
    second = jnp.logical_and(lane >= half, lane < 2 * half)
    cs, sn = jnp.cos(ang), jnp.sin(ang)
    cfac = jnp.where(jnp.logical_or(first, second), cs, 1.0)
    sfac = jnp.where(first, -sn, jnp.where(second, sn, 0.0))
    return first, cfac, sfac, half


def _partial_rope(x, factors):
    first, cfac, sfac, half = factors
    partner = jnp.where(first, pltpu.roll(x, LANES - half, axis=1), pltpu.roll(x, half, axis=1))
    return x * cfac + partner * sfac


def _dsa_prep_kernel(p_ref, tail_ref, pos_ref, inva_ref, invi_ref, knw_ref,
                     qh_ref, qih_ref, k_ref, vT_ref, ki_ref, wi_ref):
    pos = pos_ref[...].astype(F32)
    rope_a = _rope_factors(pos * inva_ref[...], ATT_ROT // 2, ATT_DIM)
    rope_i = _rope_factors(pos * invi_ref[...], IDX_ROT // 2, IDX_DIM)
    first_a, cfac_a, sfac_a, half_a = rope_a
    rope_q = (first_a, cfac_a * Q_SCALE, sfac_a * Q_SCALE, half_a)
    o_k, o_v, o_qi = ATT_D, ATT_D + ATT_DIM, ATT_D + 2 * ATT_DIM

    for h in range(ATT_HEADS):
        x = p_ref[:, h * ATT_DIM:(h + 1) * ATT_DIM].astype(F32)
        qh_ref[h] = _partial_rope(x, rope_q).astype(BF16)
    for p in range(IDX_HEADS // 2):
        x = p_ref[:, o_qi + p * LANES:o_qi + (p + 1) * LANES].astype(F32)
        y = _partial_rope(x, rope_i).astype(BF16)
        qih_ref[2 * p] = y[:, :IDX_DIM]
        qih_ref[2 * p + 1] = y[:, IDX_DIM:]

    k = p_ref[:, o_k:o_v].astype(F32)
    k_ref[...] = _partial_rope(k, rope_a).astype(BF16)
    vT_ref[...] = jnp.transpose(p_ref[:, o_v:o_qi].astype(F32)).astype(BF16)

    tail = tail_ref[...]
    lane = lax.broadcasted_iota(I32, tail.shape, 1)
    ki = jnp.where(lane < IDX_DIM, tail, 0.0)
    ms = jnp.sum(ki * ki, axis=-1, keepdims=True) * (1.0 / IDX_DIM)
    kin = ki * lax.rsqrt(ms + EPS) * knw_ref[...]
    ki_ref[...] = _partial_rope(kin, rope_i)[:, :IDX_DIM].astype(BF16)
    wi_ref[...] = tail[:, IDX_DIM:IDX_DIM + IDX_HEADS] * (IDX_HEADS ** -0.5 * IDX_DIM ** -0.5)


def _dsa_prep(proj1, tail, pos_col, knw, bsz, seq):
    t = proj1.shape[0]
    tm = PREP_TM
    per_b = seq // tm
    ha, hi = ATT_ROT // 2, IDX_ROT // 2
    inv_a = ROPE_THETA ** (-jnp.arange(ha, dtype=F32) / ha)
    inv_i = ROPE_THETA ** (-jnp.arange(hi, dtype=F32) / hi)
    lane = jnp.arange(LANES)
    inva = jnp.where(lane < ATT_ROT, inv_a[lane % ha], 0.0).reshape(1, LANES)
    invi = jnp.where(lane % IDX_DIM < IDX_ROT, inv_i[lane % hi], 0.0).reshape(1, LANES)
    knw_pad = jnp.zeros((1, LANES), F32).at[0, :IDX_DIM].set(knw)
    return pl.pallas_call(
        _dsa_prep_kernel,
        out_shape=(
            jax.ShapeDtypeStruct((ATT_HEADS, t, ATT_DIM), BF16),
            jax.ShapeDtypeStruct((IDX_HEADS, t, IDX_DIM), BF16),
            jax.ShapeDtypeStruct((t, ATT_DIM), BF16),
            jax.ShapeDtypeStruct((bsz, ATT_DIM, seq), BF16),
            jax.ShapeDtypeStruct((t, IDX_DIM), BF16),
            jax.ShapeDtypeStruct((t, IDX_HEADS), F32),
        ),
        grid=(t // tm,),
        in_specs=[
            pl.BlockSpec((tm, proj1.shape[1]), lambda i: (i, 0)),
            pl.BlockSpec((tm, LANES), lambda i: (i, 0)),
            pl.BlockSpec((tm, 1), lambda i: (i, 0)),
            pl.BlockSpec((1, LANES), lambda i: (0, 0)),
            pl.BlockSpec((1, LANES), lambda i: (0, 0)),
            pl.BlockSpec((1, LANES), lambda i: (0, 0)),
        ],
        out_specs=(
            pl.BlockSpec((ATT_HEADS, tm, ATT_DIM), lambda i: (0, i, 0)),
            pl.BlockSpec((IDX_HEADS, tm, IDX_DIM), lambda i: (0, i, 0)),
            pl.BlockSpec((tm, ATT_DIM), lambda i: (i, 0)),
            pl.BlockSpec((None, ATT_DIM, tm), lambda i: (i // per_b, 0, i % per_b)),
            pl.BlockSpec((tm, IDX_DIM), lambda i: (i, 0)),
            pl.BlockSpec((tm, IDX_HEADS), lambda i: (i, 0)),
        ),
        compiler_params=_cparams(("parallel",)),
        name="dsa_prep",
    )(proj1, tail, pos_col, inva, invi, knw_pad)


DSA_KGROUP = 512
HEAD_PAIR = 2


REDUCE_SLABS = 8


def _col_reduce(x, reduce_fn, combine_fn):
    n = x.shape[0] // REDUCE_SLABS
    parts = [reduce_fn(x[g * n:(g + 1) * n], axis=0, keepdims=True) for g in range(REDUCE_SLABS)]
    while len(parts) > 1:
        parts = [combine_fn(parts[i], parts[i + 1]) for i in range(0, len(parts), 2)]
    return parts[0]


def _col_sum(x):
    return _col_reduce(x, jnp.sum, jnp.add)


def _col_max(x):
    return _col_reduce(x, jnp.max, jnp.maximum)


def _key_to_float(key):
    return pltpu.bitcast(jnp.where(key < 0, key ^ INT_MAX, key), F32)


def _dsa_body(lk, topk, j, qh_ref, qih_ref, wiT_ref, k_ref, vT_ref, ki_ref, o_ref):
    qb = Q_BLOCK
    kidx = lax.broadcasted_iota(I32, (lk, qb), 0)
    qpos = j * qb + lax.broadcasted_iota(I32, (lk, qb), 1)
    visible = kidx <= qpos

    ki = ki_ref[0:lk, :]
    score = jnp.zeros((lk, qb), F32)
    for h in range(0, IDX_HEADS, HEAD_PAIR):
        qi = qih_ref[h:h + HEAD_PAIR].reshape(HEAD_PAIR * qb, IDX_DIM)
        s = jnp.maximum(_dot_nt(ki, qi), 0.0)
        for r in range(HEAD_PAIR):
            score = score + s[:, r * qb:(r + 1) * qb] * wiT_ref[h + r:h + r + 1, :]
    score = jnp.where(visible, score, -jnp.inf)

    def count(mask):
        return _col_sum(jnp.where(mask, 1, 0).astype(I32))

    def bit_step(i, ans):
        trial = jnp.where(i == 0, jnp.zeros_like(ans), ans | jnp.left_shift(1, 31 - i))
        return jnp.where(count(score >= _key_to_float(trial)) >= topk, trial, ans)

    thr = _key_to_float(lax.fori_loop(0, 32, bit_step, jnp.full((1, qb), INT_MIN, I32)))
    above = score > thr
    equal = score == thr
    need = topk - count(above)
    over = count(equal) > need
    few = (j * qb + lax.broadcasted_iota(I32, (1, qb), 1)) < topk

    def tie_cut():
        nbits = max(lk - 1, 1).bit_length()

        def idx_step(i, ans):
            trial = ans | jnp.left_shift(1, nbits - 1 - i)
            c = count(jnp.logical_and(equal, kidx < trial))
            return jnp.where(c < need, trial, ans)

        cut = lax.fori_loop(0, nbits, idx_step, jnp.zeros((1, qb), I32))
        return jnp.where(over, cut, INT_MAX)

    cut = lax.cond(jnp.max(over.astype(I32)) > 0, tie_cut, lambda: jnp.full((1, qb), INT_MAX, I32))
    picked = jnp.logical_or(above, jnp.logical_and(equal, kidx <= cut))
    sel = jnp.logical_and(visible, jnp.logical_or(few, picked))
    bias = jnp.where(sel, 0.0, NEG_BIG)

    k = k_ref[0:lk, :]
    vT = vT_ref[:, 0:lk]
    bias2 = jnp.concatenate([bias] * HEAD_PAIR, axis=1)
    for h in range(0, ATT_HEADS, HEAD_PAIR):
        qq = qh_ref[h:h + HEAD_PAIR].reshape(HEAD_PAIR * qb, ATT_DIM)
        s = _dot_nt(k, qq) + bias2
        m = _col_max(s)
        p = jnp.exp2(s - m)
        l = _col_sum(p)
        oT = _dot(vT, p.astype(BF16)) / l
        for r in range(HEAD_PAIR):
            o_ref[:, (h + r) * ATT_DIM:(h + r + 1) * ATT_DIM] = (
                jnp.transpose(oT[:, r * qb:(r + 1) * qb]).astype(o_ref.dtype))


def _dsa_kernel(qh_ref, qih_ref, wiT_ref, k_ref, vT_ref, ki_ref, o_ref, *, seq):
    j = pl.program_id(1)
    per_group = DSA_KGROUP // Q_BLOCK
    topk = min(TOPK, seq // 4)
    for g in range(seq // DSA_KGROUP):
        @pl.when(j // per_group == g)
        def _(g=g):
            _dsa_body((g + 1) * DSA_KGROUP, topk, j, qh_ref, qih_ref, wiT_ref, k_ref, vT_ref, ki_ref, o_ref)


def _dsa_attention(qh, qih, wiT, k_r, vT, ki_r, bsz, seq):
    t = k_r.shape[0]
    nb = seq // Q_BLOCK
    return pl.pallas_call(
        functools.partial(_dsa_kernel, seq=seq),
        out_shape=jax.ShapeDtypeStruct((t, ATT_D), BF16),
        grid=(bsz, nb),
        in_specs=[
            pl.BlockSpec((ATT_HEADS, Q_BLOCK, ATT_DIM), lambda b, j: (0, b * nb + j, 0)),
            pl.BlockSpec((IDX_HEADS, Q_BLOCK, IDX_DIM), lambda b, j: (0, b * nb + j, 0)),
            pl.BlockSpec((IDX_HEADS, Q_BLOCK), lambda b, j: (0, b * nb + j)),
            pl.BlockSpec((seq, ATT_DIM), lambda b, j: (b, 0)),
            pl.BlockSpec((None, ATT_DIM, seq), lambda b, j: (b, 0, 0)),
            pl.BlockSpec((seq, IDX_DIM), lambda b, j: (b, 0)),
        ],
        out_specs=pl.BlockSpec((Q_BLOCK, ATT_D), lambda b, j: (b * nb + j, 0)),
        compiler_params=_cparams(("parallel", "parallel")),
        name="dsa_attention",
    )(qh, qih, wiT, k_r, vT, ki_r)


def _pad_rows(w, n):
    return jnp.pad(w, ((0, n - w.shape[0]), (0, 0)))


def _hybrid_mixer(x2, mods_l, norm_w_l, pos_col, w_in, e, conv_w, conv_b, a_log, dt_bias, d_skip, ssd_norm_w,
                  ret_norm_w, w_out, bsz, seq):
    h = _norm_mod(x2, norm_w_l[2], mods_l, 1, seq)
    tm = min(MM_TM, seq)
    n_zx = SSD_D + SSD_D + 2 * SSD_BCW
    ret_lo = n_zx + SSD_HEADS
    w_t = jnp.swapaxes(w_in, 1, 2)
    proj_zx = _matmul_wt(h, w_t, BF16, tm, 1024, n_zx, e)
    proj_ret = _matmul_wt(h, w_t, BF16, tm, 1024, w_t.shape[1] - ret_lo, e, row0=ret_lo)
    w_dt = _pad_rows(w_t[e, n_zx:ret_lo], LANES)[None]
    dt_pad = _matmul_wt(h, w_dt, F32, tm, LANES, LANES, 0)
    dtT = jnp.swapaxes(dt_pad[:, :SSD_HEADS].reshape(bsz, seq, SSD_HEADS), 1, 2)
    ya = _ssd(proj_zx, dt_pad, dtT, conv_w, conv_b, a_log, dt_bias, d_skip, ssd_norm_w, bsz, seq)
    yb = _retention(proj_ret, pos_col, ret_norm_w, bsz, seq)
    return _proj_resid([ya, yb], w_out, e, x2, mods_l, 1, norm_w_l[3], seq)


def _dsa_mixer(x2, mods_l, norm_w_l, pos_col, w_in, o, knw, w_out, bsz, seq):
    h = _norm_mod(x2, norm_w_l[2], mods_l, 1, seq)
    tm = min(MM_TM, seq)
    n_main = ATT_D + 2 * ATT_DIM + IDX_D
    w_t = jnp.swapaxes(w_in, 1, 2)
    proj1 = _matmul_wt(h, w_t, BF16, tm, n_main // 2, n_main, o)
    tail = _matmul_wt(h, _pad_rows(w_t[o, n_main:], LANES)[None], F32, tm, LANES, LANES, 0)
    qh, qih, k_r, vT, ki_r, wi = _dsa_prep(proj1, tail, pos_col, knw, bsz, seq)
    att = _dsa_attention(qh, qih, wi.T, k_r, vT, ki_r, bsz, seq)
    return _proj_resid([att], w_out, o, x2, mods_l, 1, norm_w_l[3], seq)


def kernel(x, c, positions, mod_w, mod_b, norm_w, ffn_w_gate, ffn_w_up, ffn_w_down, hy_w_in, hy_conv_w,
           hy_conv_b, ssd_A_log, ssd_dt_bias, ssd_D, ssd_norm_w, ret_norm_w, hy_w_out, dsa_w_in,
           idx_k_norm_w, dsa_w_out):
    bsz, seq, d = x.shape
    depth = mod_w.shape[0]
    x2 = x.reshape(bsz * seq, d)
    pos_col = positions.reshape(bsz * seq, 1)
    mods = _mods(c, mod_w, mod_b)
    for i in range(depth):
        ml, nw = mods[i], norm_w[i]
        x2 = _ffn(x2, ml, 0, nw[0], nw[1], ffn_w_gate, ffn_w_up, ffn_w_down, i, 0, seq)
        if i % 2 == 0:
            e = i // 2
            x2 = _hybrid_mixer(x2, ml, nw, pos_col, hy_w_in, e, hy_conv_w[e], hy_conv_b[e], ssd_A_log[e],
                               ssd_dt_bias[e], ssd_D[e], ssd_norm_w[e], ret_norm_w[e], hy_w_out, bsz, seq)
        else:
            o = i // 2
            x2 = _dsa_mixer(x2, ml, nw, pos_col, dsa_w_in, o, idx_k_norm_w[o], dsa_w_out, bsz, seq)
        x2 = _ffn(x2, ml, 2, nw[4], nw[5], ffn_w_gate, ffn_w_up, ffn_w_down, i, 1, seq)
    return x2.reshape(bsz, seq, d)
```

```python
import functools
import math

import jax
import jax.numpy as jnp
from jax import lax
from jax.experimental import pallas as pl
from jax.experimental.pallas import tpu as pltpu

F32 = jnp.float32
BF16 = jnp.bfloat16
I32 = jnp.int32

EPS = 1e-6
SSD_HEADS = 32
SSD_HEAD_DIM = 64
SSD_GROUPS = 8
SSD_STATE = 128
SSD_CONV = 4
CHUNK = 128
RET_HEADS = 8
RET_DIM = 256
RET_THETA = 10000.0
ATT_HEADS = 16
ATT_DIM = 128
IDX_HEADS = 16
IDX_DIM = 64
TOPK = 256
Q_BLOCK = 128
ROPE_THETA = 500000.0
ROPE_FRACTION = 4

V7X_VMEM_BYTES = 64 * 1024 * 1024
VMEM_LIMIT = V7X_VMEM_BYTES - 8 * 1024 * 1024
LANES = 128
SUBLANES = 8

INT_MIN = -(2 ** 31)
INT_MAX = 2 ** 31 - 1
NEG_BIG = -1e30


def _cparams(sem):
    return pltpu.CompilerParams(dimension_semantics=sem, vmem_limit_bytes=VMEM_LIMIT)


def _sigmoid(x):
    return 1.0 / (1.0 + jnp.exp(-x))


def _silu(x):
    return x * _sigmoid(x)


def _softplus(x):
    return jnp.maximum(x, 0.0) + jnp.log1p(jnp.exp(-jnp.abs(x)))


def _rms(x, w):
    return x * lax.rsqrt(jnp.mean(x * x, axis=-1, keepdims=True) + EPS) * w


def _dot(a, b):
    return jnp.dot(a, b, preferred_element_type=F32)


def _dot_nt(a, b):
    return lax.dot_general(a, b, (((1,), (1,)), ((), ())), preferred_element_type=F32)


def _dot_tn(a, b):
    return lax.dot_general(a, b, (((0,), (0,)), ((), ())), preferred_element_type=F32)


def _split3(x):
    hi = x.astype(BF16)
    r = x - hi.astype(F32)
    mid = r.astype(BF16)
    lo = (r - mid.astype(F32)).astype(BF16)
    return hi, mid, lo


def _row_chunks(n_rows, rc, fn):
    def body(i, carry):
        fn(pl.ds(pl.multiple_of(i * rc, rc), rc))
        return carry
    lax.fori_loop(0, n_rows // rc, body, 0)


def _mods_kernel(c_ref, w_ref, b_ref, o_ref):
    cond = _silu(c_ref[...]).astype(BF16)
    o_ref[...] = _dot(cond, w_ref[...].astype(BF16)) + b_ref[...]


def _mods(c, mod_w, mod_b):
    depth, d, n = mod_w.shape
    bsz = c.shape[0]
    c8 = jnp.zeros((SUBLANES, d), F32).at[:bsz].set(c)
    tn = 1024
    out = pl.pallas_call(
        _mods_kernel,
        out_shape=jax.ShapeDtypeStruct((depth, SUBLANES, n), F32),
        grid=(depth, n // tn),
        in_specs=[
            pl.BlockSpec((SUBLANES, d), lambda i, j: (0, 0)),
            pl.BlockSpec((None, d, tn), lambda i, j: (i, 0, j)),
            pl.BlockSpec((None, 1, tn), lambda i, j: (i, 0, j)),
        ],
        out_specs=pl.BlockSpec((None, SUBLANES, tn), lambda i, j: (i, 0, j)),
        compiler_params=_cparams(("parallel", "parallel")),
        name="mods",
    )(c8, mod_w, mod_b.reshape(depth, 1, n))
    return out[:, :bsz].reshape(depth, bsz * 3, 3, d)


def _norm_mod_kernel(x_ref, w_ref, m_ref, o_ref):
    h = _rms(x_ref[...], w_ref[...]) * (1.0 + m_ref[1:2, :]) + m_ref[0:1, :]
    o_ref[...] = h.astype(o_ref.dtype)


def _norm_mod(x2, pre_w, mods_l, sub, seq):
    t, d = x2.shape
    tm = 256
    per_b = seq // tm
    return pl.pallas_call(
        _norm_mod_kernel,
        out_shape=jax.ShapeDtypeStruct((t, d), BF16),
        grid=(t // tm,),
        in_specs=[
            pl.BlockSpec((tm, d), lambda i: (i, 0)),
            pl.BlockSpec((1, d), lambda i: (0, 0)),
            pl.BlockSpec((None, 3, d), lambda i: ((i // per_b) * 3 + sub, 0, 0)),
        ],
        out_specs=pl.BlockSpec((tm, d), lambda i: (i, 0)),
        compiler_params=_cparams(("parallel",)),
        name="norm_mod",
    )(x2, pre_w.reshape(1, d), mods_l)


MM_TM = 1024


def _mm_kernel(a_ref, w_ref, o_ref):
    o_ref[...] = _dot_nt(a_ref[...], w_ref[...].astype(BF16)).astype(o_ref.dtype)


def _matmul_wt(a, w_t, out_dtype, tm, tn, n_out, lead, row0=0):
    m, k = a.shape
    if row0 % tn == 0:
        w_spec = pl.BlockSpec((None, tn, k), lambda i, j: (lead, row0 // tn + j, 0))
    else:
        w_t = w_t[lead]
        assert row0 % SUBLANES == 0 and tn % SUBLANES == 0
        w_spec = pl.BlockSpec((pl.Element(tn), pl.Element(k)),
                              lambda i, j: ((row0 // SUBLANES + j * (tn // SUBLANES)) * SUBLANES, 0))
    return pl.pallas_call(
        _mm_kernel,
        out_shape=jax.ShapeDtypeStruct((m, n_out), out_dtype),
        grid=(m // tm, n_out // tn),
        in_specs=[pl.BlockSpec((tm, k), lambda i, j: (i, 0)), w_spec],
        out_specs=pl.BlockSpec((tm, tn), lambda i, j: (i, j)),
        compiler_params=_cparams(("parallel", "parallel")),
        name="matmul",
    )(a, w_t)


FFN_TM = 1024
FFN_TF = 512
FFN_SUB = 256
ROW_CHUNK = 128


def _ffn_kernel(x_ref, m_ref, prew_ref, postw_ref, wg_ref, wu_ref, wd_ref, o_ref, h_ref, *, nf, tm, tf):
    f = pl.program_id(1)

    @pl.when(f == 0)
    def _():
        def pro(rows):
            h = _rms(x_ref[rows, :], prew_ref[...]) * (1.0 + m_ref[1:2, :]) + m_ref[0:1, :]
            h_ref[rows, :] = h.astype(BF16)
            o_ref[rows, :] = jnp.zeros((ROW_CHUNK, o_ref.shape[1]), F32)
        _row_chunks(tm, ROW_CHUNK, pro)

    h = h_ref[...]
    for c in range(tf // FFN_SUB):
        cs = slice(c * FFN_SUB, (c + 1) * FFN_SUB)
        g = _dot(h, wg_ref[:, cs].astype(BF16))
        u = _dot(h, wu_ref[:, cs].astype(BF16))
        a = (_silu(g) * u).astype(BF16)
        o_ref[...] += _dot(a, wd_ref[cs, :].astype(BF16))

    @pl.when(f == nf - 1)
    def _():
        def epi(rows):
            y = _rms(o_ref[rows, :], postw_ref[...])
            o_ref[rows, :] = x_ref[rows, :] + 0.5 * m_ref[2:3, :] * y
        _row_chunks(tm, ROW_CHUNK, epi)


def _ffn(x2, mods_l, sub, pre_w, post_w, w_gate, w_up, w_down, layer, which, seq):
    t, d = x2.shape
    dff = w_gate.shape[-1]
    tm, tf = min(FFN_TM, seq), FFN_TF
    nf = dff // tf
    per_b = seq // tm
    one = pl.Buffered(1)
    return pl.pallas_call(
        functools.partial(_ffn_kernel, nf=nf, tm=tm, tf=tf),
        out_shape=jax.ShapeDtypeStruct((t, d), F32),
        grid=(t // tm, nf),
        in_specs=[
            pl.BlockSpec((tm, d), lambda i, f: (i, 0), pipeline_mode=one),
            pl.BlockSpec((None, 3, d), lambda i, f: ((i // per_b) * 3 + sub, 0, 0)),
            pl.BlockSpec((1, d), lambda i, f: (0, 0)),
            pl.BlockSpec((1, d), lambda i, f: (0, 0)),
            pl.BlockSpec((None, None, d, tf), lambda i, f: (layer, which, 0, f)),
            pl.BlockSpec((None, None, d, tf), lambda i, f: (layer, which, 0, f)),
            pl.BlockSpec((None, None, tf, d), lambda i, f: (layer, which, f, 0)),
        ],
        out_specs=pl.BlockSpec((tm, d), lambda i, f: (i, 0), pipeline_mode=one),
        scratch_shapes=[pltpu.VMEM((tm, d), BF16)],
        compiler_params=_cparams(("parallel", "arbitrary")),
        name="ffn",
    )(x2, mods_l, pre_w.reshape(1, d), post_w.reshape(1, d), w_gate, w_up, w_down)


PROJ_TM = 512


def _proj_resid_kernel(*refs, n_src, tm):
    a_refs = refs[:n_src]
    w_ref, x_ref, m_ref, postw_ref, o_ref = refs[n_src:]
    ks = a_refs[0].shape[1]
    acc = _dot(a_refs[0][...], w_ref[0:ks, :])
    for s in range(1, n_src):
        acc = acc + _dot(a_refs[s][...], w_ref[s * ks:(s + 1) * ks, :])
    o_ref[...] = acc

    def epi(rows):
        y = _rms(o_ref[rows, :], postw_ref[...])
        o_ref[rows, :] = x_ref[rows, :] + m_ref[2:3, :] * y
    _row_chunks(tm, ROW_CHUNK, epi)


def _proj_resid(srcs, w_bf, x2, mods_l, sub, post_w, seq):
    t, d = x2.shape
    n_src = len(srcs)
    ks = srcs[0].shape[1]
    tm = min(PROJ_TM, seq)
    per_b = seq // tm
    return pl.pallas_call(
        functools.partial(_proj_resid_kernel, n_src=n_src, tm=tm),
        out_shape=jax.ShapeDtypeStruct((t, d), F32),
        grid=(t // tm,),
        in_specs=[pl.BlockSpec((tm, ks), lambda i: (i, 0)) for _ in range(n_src)] + [
            pl.BlockSpec((n_src * ks, d), lambda i: (0, 0), pipeline_mode=pl.Buffered(1)),
            pl.BlockSpec((tm, d), lambda i: (i, 0)),
            pl.BlockSpec((None, 3, d), lambda i: ((i // per_b) * 3 + sub, 0, 0)),
            pl.BlockSpec((1, d), lambda i: (0, 0)),
        ],
        out_specs=pl.BlockSpec((tm, d), lambda i: (i, 0)),
        compiler_params=_cparams(("parallel",)),
        name="proj_resid",
    )(*srcs, w_bf, x2, mods_l, post_w.reshape(1, d))


SSD_D = SSD_HEADS * SSD_HEAD_DIM
SSD_GW = SSD_D // SSD_GROUPS
SSD_BCW = SSD_GROUPS * SSD_STATE


def _ssd_kernel(z_ref, xs_ref, bc_ref, xsp_ref, bcp_ref, dt_ref, dtT_ref, cw_ref, cb_ref, alr_ref, alc_ref,
                dbr_ref, dbc_ref, dexp_ref, nw_ref, e_ref, o_ref, s_ref):
    c = pl.program_id(1)
    q = CHUNK

    @pl.when(c == 0)
    def _():
        s_ref[...] = jnp.zeros_like(s_ref)

    n_shift = SSD_CONV - 1
    tt = lax.broadcasted_iota(I32, (n_shift * q, 2 * q), 0)
    rr = lax.broadcasted_iota(I32, (n_shift * q, 2 * q), 1)
    shift_mat = (rr == (tt % q) + q - n_shift + tt // q).astype(BF16)

    def conv(cur_ref, prev_ref, lo, hi):
        cur = cur_ref[...]
        prev = jnp.where(c > 0, prev_ref[...], jnp.zeros_like(cur))
        delayed = _dot(shift_mat, jnp.concatenate([prev, cur], axis=0))
        acc = cb_ref[:, lo:hi] + cw_ref[n_shift:n_shift + 1, lo:hi] * cur.astype(F32)
        for k in range(n_shift):
            acc = acc + cw_ref[k:k + 1, lo:hi] * delayed[k * q:(k + 1) * q]
        return _silu(acc)

    xs = conv(xs_ref, xsp_ref, 0, SSD_D)
    bc = conv(bc_ref, bcp_ref, SSD_D, SSD_D + 2 * SSD_BCW)

    rows = lax.broadcasted_iota(I32, (q, q), 0)
    cols = lax.broadcasted_iota(I32, (q, q), 1)
    causal = rows >= cols
    tril = causal.astype(BF16)
    triu = (rows <= cols).astype(BF16)

    dt = _softplus(dt_ref[:, 0:SSD_HEADS] + dbr_ref[...])
    dA = dt * (-jnp.exp(alr_ref[...]))
    cum = sum(_dot(tril, p) for p in _split3(dA))
    dtT = _softplus(dtT_ref[...] + dbc_ref[...])
    dAT = dtT * (-jnp.exp(alc_ref[...]))
    cumT = sum(_dot(p, triu) for p in _split3(dAT))

    ecum = jnp.exp(cum)
    decs = jnp.exp(cum[q - 1:q, :] - cum)
    stack = jnp.concatenate([dt, ecum, decs], axis=0)
    ex = sum(_dot(p, e_ref[...]) for p in _split3(stack))
    dt_x, ecum_x, decs_x = ex[0:q], ex[q:2 * q], ex[2 * q:3 * q]
    cdec_x = ecum_x[q - 1:q, :]

    xdt = xs * dt_x
    xw = (xdt * decs_x).astype(BF16)
    xdt_b = xdt.astype(BF16)

    y_parts = []
    for g in range(SSD_GROUPS):
        b_g = bc[:, g * SSD_STATE:(g + 1) * SSD_STATE].astype(BF16)
        c_g = bc[:, SSD_BCW + g * SSD_STATE:SSD_BCW + (g + 1) * SSD_STATE].astype(BF16)
        gs = slice(g * SSD_GW, (g + 1) * SSD_GW)
        cb = _dot_nt(c_g, b_g)
        s_g = s_ref[g]
        y_g = _dot(c_g, s_g.astype(BF16)) * ecum_x[:, gs]
        diag = []
        for r in range(SSD_HEADS // SSD_GROUPS):
            hh = g * (SSD_HEADS // SSD_GROUPS) + r
            seg = cum[:, hh:hh + 1] - cumT[hh:hh + 1, :]
            dec = jnp.where(causal, jnp.exp(seg), 0.0)
            m = (cb * dec).astype(BF16)
            hs = slice(g * SSD_GW + r * SSD_HEAD_DIM, g * SSD_GW + (r + 1) * SSD_HEAD_DIM)
            diag.append(_dot(m, xdt_b[:, hs]))
        y_parts.append(y_g + jnp.concatenate(diag, axis=1))
        s_ref[g] = s_g * cdec_x[:, gs] + _dot_tn(b_g, xw[:, gs])
    y = jnp.concatenate(y_parts, axis=1) + xs * dexp_ref[...]
    o_ref[...] = _rms(y * _silu(z_ref[...].astype(F32)), nw_ref[...]).astype(o_ref.dtype)


def _ssd(proj0, dt_pad, dtT, conv_w, conv_b, a_log, dt_bias, d_skip, norm_w, bsz, seq):
    t = proj0.shape[0]
    nc = seq // CHUNK
    h = SSD_HEADS
    expand = jnp.repeat(jnp.eye(h, dtype=BF16), SSD_HEAD_DIM, axis=1)
    dexp = jnp.repeat(d_skip.astype(F32), SSD_HEAD_DIM).reshape(1, SSD_D)
    full = lambda shape: pl.BlockSpec(shape, lambda b, c: (0,) * len(shape))
    col = lambda j: pl.BlockSpec((CHUNK, SSD_D), lambda b, c: (b * nc + c, j))
    prev = lambda j: pl.BlockSpec((CHUNK, SSD_D), lambda b, c: (b * nc + jnp.maximum(c - 1, 0), j))
    return pl.pallas_call(
        _ssd_kernel,
        out_shape=jax.ShapeDtypeStruct((t, SSD_D), BF16),
        grid=(bsz, nc),
        in_specs=[
            col(0), col(1), col(2), prev(1), prev(2),
            pl.BlockSpec((CHUNK, LANES), lambda b, c: (b * nc + c, 0)),
            pl.BlockSpec((None, h, CHUNK), lambda b, c: (b, 0, c)),
            full((SSD_CONV, SSD_D + 2 * SSD_BCW)), full((1, SSD_D + 2 * SSD_BCW)),
            full((1, h)), full((h, 1)), full((1, h)), full((h, 1)),
            full((1, SSD_D)), full((1, SSD_D)), full((h, SSD_D)),
        ],
        out_specs=pl.BlockSpec((CHUNK, SSD_D), lambda b, c: (b * nc + c, 0)),
        scratch_shapes=[pltpu.VMEM((SSD_GROUPS, SSD_STATE, SSD_GW), F32)],
        compiler_params=_cparams(("parallel", "arbitrary")),
        name="ssd",
    )(proj0, proj0, proj0, proj0, proj0, dt_pad, dtT, conv_w, conv_b.reshape(1, -1),
      a_log.reshape(1, h), a_log.reshape(h, 1), dt_bias.reshape(1, h), dt_bias.reshape(h, 1),
      dexp, norm_w.reshape(1, SSD_D), expand)


RET_D = RET_HEADS * RET_DIM


def _ret_kernel(q_ref, k_ref, v_ref, g_ref, pos_ref, inv_ref, nw_ref, o_ref, r_ref):
    c = pl.program_id(1)
    q = CHUNK
    half = RET_DIM // 2

    @pl.when(c == 0)
    def _():
        r_ref[...] = jnp.zeros_like(r_ref)

    ang = pos_ref[...].astype(F32) * inv_ref[...]
    cs, sn = jnp.cos(ang), jnp.sin(ang)
    li = lax.broadcasted_iota(I32, (q, q), 0)
    si = lax.broadcasted_iota(I32, (q, q), 1)
    dist = (li - si).astype(F32)
    lcol = lax.broadcasted_iota(I32, (q, 1), 0).astype(F32)

    def rope(x):
        x1, x2 = x[:, :half], x[:, half:]
        return jnp.concatenate([x1 * cs - x2 * sn, x1 * sn + x2 * cs], axis=1)

    for h in range(RET_HEADS):
        lg = math.log(1.0 - 2.0 ** (-5.0 - h))
        hs = slice(h * RET_DIM, (h + 1) * RET_DIM)
        qr = rope(q_ref[:, hs].astype(F32)).astype(BF16)
        kr = rope(k_ref[:, hs].astype(F32)) * (RET_DIM ** -0.5)
        v = v_ref[:, hs]
        intra = jnp.where(dist >= 0.0, jnp.exp(dist * lg), 0.0)
        sc = _dot_nt(qr, kr.astype(BF16)) * intra
        r_h = r_ref[h]
        y = _dot(sc.astype(BF16), v) + _dot(qr, r_h.astype(BF16)) * jnp.exp((lcol + 1.0) * lg)
        kd = (kr * jnp.exp((q - 1.0 - lcol) * lg)).astype(BF16)
        r_ref[h] = r_h * math.exp(q * lg) + _dot_tn(kd, v)
        gate = _silu(g_ref[:, hs].astype(F32))
        o_ref[:, hs] = (gate * _rms(y, nw_ref[:, hs])).astype(o_ref.dtype)


def _retention(proj0, pos_col, norm_w, bsz, seq):
    t = proj0.shape[0]
    nc = seq // CHUNK
    half = RET_DIM // 2
    inv = (RET_THETA ** (-jnp.arange(half, dtype=F32) / half)).reshape(1, half)
    col = lambda j: pl.BlockSpec((CHUNK, RET_D), lambda b, c: (b * nc + c, j))
    return pl.pallas_call(
        _ret_kernel,
        out_shape=jax.ShapeDtypeStruct((t, RET_D), BF16),
        grid=(bsz, nc),
        in_specs=[
            col(0), col(1), col(2), col(3),
            pl.BlockSpec((CHUNK, 1), lambda b, c: (b * nc + c, 0)),
            pl.BlockSpec((1, half), lambda b, c: (0, 0)),
            pl.BlockSpec((1, RET_D), lambda b, c: (0, 0)),
        ],
        out_specs=pl.BlockSpec((CHUNK, RET_D), lambda b, c: (b * nc + c, 0)),
        scratch_shapes=[pltpu.VMEM((RET_HEADS, RET_DIM, RET_DIM), F32)],
        compiler_params=_cparams(("parallel", "arbitrary")),
        name="retention",
    )(proj0, proj0, proj0, proj0, pos_col, inv, norm_w.reshape(1, RET_D))


ATT_D = ATT_HEADS * ATT_DIM
IDX_D = IDX_HEADS * IDX_DIM
ATT_ROT = ATT_DIM // ROPE_FRACTION
IDX_ROT = IDX_DIM // ROPE_FRACTION
PREP_TM = 256
Q_SCALE = ATT_DIM ** -0.5 * math.log2(math.e)


def _rope_factors(ang, half, period):
    lane = lax.broadcasted_iota(I32, ang.shape, 1) % period
    first = lane < half
    second = jnp.logical_and(lane >= half, lane < 2 * half)
    cs, sn = jnp.cos(ang), jnp.sin(ang)
    cfac = jnp.where(jnp.logical_or(first, second), cs, 1.0)
    sfac = jnp.where(first, -sn, jnp.where(second, sn, 0.0))
    return first, cfac, sfac, half


def _partial_rope(x, factors):
    first, cfac, sfac, half = factors
    partner = jnp.where(first, pltpu.roll(x, LANES - half, axis=1), pltpu.roll(x, half, axis=1))
    return x * cfac + partner * sfac


def _dsa_prep_kernel(p_ref, tail_ref, pos_ref, inva_ref, invi_ref, knw_ref,
                     qh_ref, qih_ref, k_ref, vT_ref, ki_ref, wi_ref):
    pos = pos_ref[...].astype(F32)
    rope_a = _rope_factors(pos * inva_ref[...], ATT_ROT // 2, ATT_DIM)
    rope_i = _rope_factors(pos * invi_ref[...], IDX_ROT // 2, IDX_DIM)
    first_a, cfac_a, sfac_a, half_a = rope_a
    rope_q = (first_a, cfac_a * Q_SCALE, sfac_a * Q_SCALE, half_a)
    o_k, o_v, o_qi = ATT_D, ATT_D + ATT_DIM, ATT_D + 2 * ATT_DIM

    for h in range(ATT_HEADS):
        x = p_ref[:, h * ATT_DIM:(h + 1) * ATT_DIM].astype(F32)
        qh_ref[h] = _partial_rope(x, rope_q).astype(BF16)
    for p in range(IDX_HEADS // 2):
        x = p_ref[:, o_qi + p * LANES:o_qi + (p + 1) * LANES].astype(F32)
        y = _partial_rope(x, rope_i).astype(BF16)
        qih_ref[2 * p] = y[:, :IDX_DIM]
        qih_ref[2 * p + 1] = y[:, IDX_DIM:]

    k = p_ref[:, o_k:o_v].astype(F32)
    k_ref[...] = _partial_rope(k, rope_a).astype(BF16)
    vT_ref[...] = jnp.transpose(p_ref[:, o_v:o_qi].astype(F32)).astype(BF16)

    tail = tail_ref[...]
    lane = lax.broadcasted_iota(I32, tail.shape, 1)
    ki = jnp.where(lane < IDX_DIM, tail, 0.0)
    ms = jnp.sum(ki * ki, axis=-1, keepdims=True) * (1.0 / IDX_DIM)
    kin = ki * lax.rsqrt(ms + EPS) * knw_ref[...]
    ki_ref[...] = _partial_rope(kin, rope_i)[:, :IDX_DIM].astype(BF16)
    wi_ref[...] = tail[:, IDX_DIM:IDX_DIM + IDX_HEADS] * (IDX_HEADS ** -0.5 * IDX_DIM ** -0.5)


def _dsa_prep(proj1, tail, pos_col, knw, bsz, seq):
    t = proj1.shape[0]
    tm = PREP_TM
    per_b = seq // tm
    ha, hi = ATT_ROT // 2, IDX_ROT // 2
    inv_a = ROPE_THETA ** (-jnp.arange(ha, dtype=F32) / ha)
    inv_i = ROPE_THETA ** (-jnp.arange(hi, dtype=F32) / hi)
    lane = jnp.arange(LANES)
    inva = jnp.where(lane < ATT_ROT, inv_a[lane % ha], 0.0).reshape(1, LANES)
    invi = jnp.where(lane % IDX_DIM < IDX_ROT, inv_i[lane % hi], 0.0).reshape(1, LANES)
    knw_pad = jnp.zeros((1, LANES), F32).at[0, :IDX_DIM].set(knw)
    return pl.pallas_call(
        _dsa_prep_kernel,
        out_shape=(
            jax.ShapeDtypeStruct((ATT_HEADS, t, ATT_DIM), BF16),
            jax.ShapeDtypeStruct((IDX_HEADS, t, IDX_DIM), BF16),
            jax.ShapeDtypeStruct((t, ATT_DIM), BF16),
            jax.ShapeDtypeStruct((bsz, ATT_DIM, seq), BF16),
            jax.ShapeDtypeStruct((t, IDX_DIM), BF16),
            jax.ShapeDtypeStruct((t, IDX_HEADS), F32),
        ),
        grid=(t // tm,),
        in_specs=[
            pl.BlockSpec((tm, proj1.shape[1]), lambda i: (i, 0)),
            pl.BlockSpec((tm, LANES), lambda i: (i, 0)),
            pl.BlockSpec((tm, 1), lambda i: (i, 0)),
            pl.BlockSpec((1, LANES), lambda i: (0, 0)),
            pl.BlockSpec((1, LANES), lambda i: (0, 0)),
            pl.BlockSpec((1, LANES), lambda i: (0, 0)),
        ],
        out_specs=(
            pl.BlockSpec((ATT_HEADS, tm, ATT_DIM), lambda i: (0, i, 0)),
            pl.BlockSpec((IDX_HEADS, tm, IDX_DIM), lambda i: (0, i, 0)),
            pl.BlockSpec((tm, ATT_DIM), lambda i: (i, 0)),
            pl.BlockSpec((None, ATT_DIM, tm), lambda i: (i // per_b, 0, i % per_b)),
            pl.BlockSpec((tm, IDX_DIM), lambda i: (i, 0)),
            pl.BlockSpec((tm, IDX_HEADS), lambda i: (i, 0)),
        ),
        compiler_params=_cparams(("parallel",)),
        name="dsa_prep",
    )(proj1, tail, pos_col, inva, invi, knw_pad)


DSA_KGROUP = 512
HEAD_PAIR = 2


REDUCE_SLABS = 8


def _col_reduce(x, reduce_fn, combine_fn):
    n = x.shape[0] // REDUCE_SLABS
    parts = [reduce_fn(x[g * n:(g + 1) * n], axis=0, keepdims=True) for g in range(REDUCE_SLABS)]
    while len(parts) > 1:
        parts = [combine_fn(parts[i], parts[i + 1]) for i in range(0, len(parts), 2)]
    return parts[0]


def _col_sum(x):
    return _col_reduce(x, jnp.sum, jnp.add)


def _col_max(x):
    return _col_reduce(x, jnp.max, jnp.maximum)


def _key_to_float(key):
    return pltpu.bitcast(jnp.where(key < 0, key ^ INT_MAX, key), F32)


def _dsa_body(lk, topk, j, qh_ref, qih_ref, wiT_ref, k_ref, vT_ref, ki_ref, o_ref):
    qb = Q_BLOCK
    kidx = lax.broadcasted_iota(I32, (lk, qb), 0)
    qpos = j * qb + lax.broadcasted_iota(I32, (lk, qb), 1)
    visible = kidx <= qpos

    ki = ki_ref[0:lk, :]
    score = jnp.zeros((lk, qb), F32)
    for h in range(0, IDX_HEADS, HEAD_PAIR):
        qi = qih_ref[h:h + HEAD_PAIR].reshape(HEAD_PAIR * qb, IDX_DIM)
        s = jnp.maximum(_dot_nt(ki, qi), 0.0)
        for r in range(HEAD_PAIR):
            score = score + s[:, r * qb:(r + 1) * qb] * wiT_ref[h + r:h + r + 1, :]
    score = jnp.where(visible, score, -jnp.inf)

    def count(mask):
        return _col_sum(jnp.where(mask, 1, 0).astype(I32))

    def bit_step(i, ans):
        trial = jnp.where(i == 0, jnp.zeros_like(ans), ans | jnp.left_shift(1, 31 - i))
        return jnp.where(count(score >= _key_to_float(trial)) >= topk, trial, ans)

    thr = _key_to_float(lax.fori_loop(0, 32, bit_step, jnp.full((1, qb), INT_MIN, I32)))
    above = score > thr
    equal = score == thr
    need = topk - count(above)
    over = count(equal) > need
    few = (j * qb + lax.broadcasted_iota(I32, (1, qb), 1)) < topk

    def tie_cut():
        nbits = max(lk - 1, 1).bit_length()

        def idx_step(i, ans):
            trial = ans | jnp.left_shift(1, nbits - 1 - i)
            c = count(jnp.logical_and(equal, kidx < trial))
            return jnp.where(c < need, trial, ans)

        cut = lax.fori_loop(0, nbits, idx_step, jnp.zeros((1, qb), I32))
        return jnp.where(over, cut, INT_MAX)

    cut = lax.cond(jnp.max(over.astype(I32)) > 0, tie_cut, lambda: jnp.full((1, qb), INT_MAX, I32))
    picked = jnp.logical_or(above, jnp.logical_and(equal, kidx <= cut))
    sel = jnp.logical_and(visible, jnp.logical_or(few, picked))
    bias = jnp.where(sel, 0.0, NEG_BIG)

    k = k_ref[0:lk, :]
    vT = vT_ref[:, 0:lk]
    bias2 = jnp.concatenate([bias] * HEAD_PAIR, axis=1)
    for h in range(0, ATT_HEADS, HEAD_PAIR):
        qq = qh_ref[h:h + HEAD_PAIR].reshape(HEAD_PAIR * qb, ATT_DIM)
        s = _dot_nt(k, qq) + bias2
        m = _col_max(s)
        p = jnp.exp2(s - m)
        l = _col_sum(p)
        oT = _dot(vT, p.astype(BF16)) / l
        for r in range(HEAD_PAIR):
            o_ref[:, (h + r) * ATT_DIM:(h + r + 1) * ATT_DIM] = (
                jnp.transpose(oT[:, r * qb:(r + 1) * qb]).astype(o_ref.dtype))


def _dsa_kernel(qh_ref, qih_ref, wiT_ref, k_ref, vT_ref, ki_ref, o_ref, *, seq):
    j = pl.program_id(1)
    per_group = DSA_KGROUP // Q_BLOCK
    topk = min(TOPK, seq // 4)
    for g in range(seq // DSA_KGROUP):
        @pl.when(j // per_group == g)
        def _(g=g):
            _dsa_body((g + 1) * DSA_KGROUP, topk, j, qh_ref, qih_ref, wiT_ref, k_ref, vT_ref, ki_ref, o_ref)


def _dsa_attention(qh, qih, wiT, k_r, vT, ki_r, bsz, seq):
    t = k_r.shape[0]
    nb = seq // Q_BLOCK
    return pl.pallas_call(
        functools.partial(_dsa_kernel, seq=seq),
        out_shape=jax.ShapeDtypeStruct((t, ATT_D), BF16),
        grid=(bsz, nb),
        in_specs=[
            pl.BlockSpec((ATT_HEADS, Q_BLOCK, ATT_DIM), lambda b, j: (0, b * nb + j, 0)),
            pl.BlockSpec((IDX_HEADS, Q_BLOCK, IDX_DIM), lambda b, j: (0, b * nb + j, 0)),
            pl.BlockSpec((IDX_HEADS, Q_BLOCK), lambda b, j: (0, b * nb + j)),
            pl.BlockSpec((seq, ATT_DIM), lambda b, j: (b, 0)),
            pl.BlockSpec((None, ATT_DIM, seq), lambda b, j: (b, 0, 0)),
            pl.BlockSpec((seq, IDX_DIM), lambda b, j: (b, 0)),
        ],
        out_specs=pl.BlockSpec((Q_BLOCK, ATT_D), lambda b, j: (b * nb + j, 0)),
        compiler_params=_cparams(("parallel", "parallel")),
        name="dsa_attention",
    )(qh, qih, wiT, k_r, vT, ki_r)


def _pad_rows(w, n):
    return jnp.pad(w, ((0, n - w.shape[0]), (0, 0)))


def _hybrid_mixer(x2, mods_l, norm_w_l, pos_col, w_in, e, conv_w, conv_b, a_log, dt_bias, d_skip, ssd_norm_w,
                  ret_norm_w, w_out, bsz, seq):
    h = _norm_mod(x2, norm_w_l[2], mods_l, 1, seq)
    tm = min(MM_TM, seq)
    n_zx = SSD_D + SSD_D + 2 * SSD_BCW
    ret_lo = n_zx + SSD_HEADS
    w_t = jnp.swapaxes(w_in, 1, 2)
    proj_zx = _matmul_wt(h, w_t, BF16, tm, 1024, n_zx, e)
    proj_ret = _matmul_wt(h, w_t, BF16, tm, 1024, w_t.shape[1] - ret_lo, e, row0=ret_lo)
    w_dt = _pad_rows(w_t[e, n_zx:ret_lo], LANES)[None]
    dt_pad = _matmul_wt(h, w_dt, F32, tm, LANES, LANES, 0)
    dtT = jnp.swapaxes(dt_pad[:, :SSD_HEADS].reshape(bsz, seq, SSD_HEADS), 1, 2)
    ya = _ssd(proj_zx, dt_pad, dtT, conv_w, conv_b, a_log, dt_bias, d_skip, ssd_norm_w, bsz, seq)
    yb = _retention(proj_ret, pos_col, ret_norm_w, bsz, seq)
    return _proj_resid([ya, yb], w_out[e].astype(BF16), x2, mods_l, 1, norm_w_l[3], seq)


def _dsa_mixer(x2, mods_l, norm_w_l, pos_col, w_in, o, knw, w_out, bsz, seq):
    h = _norm_mod(x2, norm_w_l[2], mods_l, 1, seq)
    tm = min(MM_TM, seq)
    n_main = ATT_D + 2 * ATT_DIM + IDX_D
    w_t = jnp.swapaxes(w_in, 1, 2)
    proj1 = _matmul_wt(h, w_t, BF16, tm, n_main // 2, n_main, o)
    tail = _matmul_wt(h, _pad_rows(w_t[o, n_main:], LANES)[None], F32, tm, LANES, LANES, 0)
    qh, qih, k_r, vT, ki_r, wi = _dsa_prep(proj1, tail, pos_col, knw, bsz, seq)
    att = _dsa_attention(qh, qih, wi.T, k_r, vT, ki_r, bsz, seq)
    return _proj_resid([att], w_out[o].astype(BF16), x2, mods_l, 1, norm_w_l[3], seq)


def kernel(x, c, positions, mod_w, mod_b, norm_w, ffn_w_gate, ffn_w_up, ffn_w_down, hy_w_in, hy_conv_w,
           hy_conv_b, ssd_A_log, ssd_dt_bias, ssd_D, ssd_norm_w, ret_norm_w, hy_w_out, dsa_w_in,
           idx_k_norm_w, dsa_w_out):
    bsz, seq, d = x.shape
    depth = mod_w.shape[0]
    x2 = x.reshape(bsz * seq, d)
    pos_col = positions.reshape(bsz * seq, 1)
    mods = _mods(c, mod_w, mod_b)
    for i in range(depth):
        ml, nw = mods[i], norm_w[i]
        x2 = _ffn(x2, ml, 0, nw[0], nw[1], ffn_w_gate, ffn_w_up, ffn_w_down, i, 0, seq)
        if i % 2 == 0:
            e = i // 2
            x2 = _hybrid_mixer(x2, ml, nw, pos_col, hy_w_in, e, hy_conv_w[e], hy_conv_b[e], ssd_A_log[e],
                               ssd_dt_bias[e], ssd_D[e], ssd_norm_w[e], ret_norm_w[e], hy_w_out, bsz, seq)
        else:
            o = i // 2
            x2 = _dsa_mixer(x2, ml, nw, pos_col, dsa_w_in, o, idx_k_norm_w[o], dsa_w_out, bsz, seq)
        x2 = _ffn(x2, ml, 2, nw[4], nw[5], ffn_w_gate, ffn_w_up, ffn_w_down, i, 1, seq)
    return x2.reshape(bsz, seq, d)
```

```python
import functools
import math

import jax
import jax.numpy as jnp
from jax import lax
from jax.experimental import pallas as pl
from jax.experimental.pallas import tpu as pltpu

F32 = jnp.float32
BF16 = jnp.bfloat16
I32 = jnp.int32

EPS = 1e-6
SSD_HEADS = 32
SSD_HEAD_DIM = 64
SSD_GROUPS = 8
SSD_STATE = 128
SSD_CONV = 4
CHUNK = 128
RET_HEADS = 8
RET_DIM = 256
RET_THETA = 10000.0
ATT_HEADS = 16
ATT_DIM = 128
IDX_HEADS = 16
IDX_DIM = 64
TOPK = 256
Q_BLOCK = 128
ROPE_THETA = 500000.0
ROPE_FRACTION = 4

V7X_VMEM_BYTES = 64 * 1024 * 1024
VMEM_LIMIT = V7X_VMEM_BYTES - 8 * 1024 * 1024
LANES = 128
SUBLANES = 8

INT_MIN = -(2 ** 31)
INT_MAX = 2 ** 31 - 1
NEG_BIG = -1e30


def _cparams(sem):
    return pltpu.CompilerParams(dimension_semantics=sem, vmem_limit_bytes=VMEM_LIMIT)


def _sigmoid(x):
    return 1.0 / (1.0 + jnp.exp(-x))


def _silu(x):
    return x * _sigmoid(x)


def _softplus(x):
    return jnp.maximum(x, 0.0) + jnp.log1p(jnp.exp(-jnp.abs(x)))


def _rms(x, w):
    return x * lax.rsqrt(jnp.mean(x * x, axis=-1, keepdims=True) + EPS) * w


def _dot(a, b):
    return jnp.dot(a, b, preferred_element_type=F32)


def _dot_nt(a, b):
    return lax.dot_general(a, b, (((1,), (1,)), ((), ())), preferred_element_type=F32)


def _dot_tn(a, b):
    return lax.dot_general(a, b, (((0,), (0,)), ((), ())), preferred_element_type=F32)


def _split3(x):
    hi = x.astype(BF16)
    r = x - hi.astype(F32)
    mid = r.astype(BF16)
    lo = (r - mid.astype(F32)).astype(BF16)
    return hi, mid, lo


def _row_chunks(n_rows, rc, fn):
    def body(i, carry):
        fn(pl.ds(pl.multiple_of(i * rc, rc), rc))
        return carry
    lax.fori_loop(0, n_rows // rc, body, 0)


ROW_CHUNK = 128


def _inv_rms_rows(src_ref, r_ref, n_rows):
    def body(rows):
        x = src_ref[rows, :]
        r_ref[rows, :] = lax.rsqrt(jnp.mean(x * x, axis=-1, keepdims=True) + EPS)
    _row_chunks(n_rows, ROW_CHUNK, body)


def _mods_kernel(c_ref, w_ref, b_ref, o_ref):
    cond = _silu(c_ref[...]).astype(BF16)
    o_ref[...] = _dot(cond, w_ref[...].astype(BF16)) + b_ref[...]


def _mods(c, mod_w, mod_b):
    depth, d, n = mod_w.shape
    bsz = c.shape[0]
    c8 = jnp.zeros((SUBLANES, d), F32).at[:bsz].set(c)
    tn = 1024
    out = pl.pallas_call(
        _mods_kernel,
        out_shape=jax.ShapeDtypeStruct((depth, SUBLANES, n), F32),
        grid=(depth, n // tn),
        in_specs=[
            pl.BlockSpec((SUBLANES, d), lambda i, j: (0, 0)),
            pl.BlockSpec((None, d, tn), lambda i, j: (i, 0, j)),
            pl.BlockSpec((None, 1, tn), lambda i, j: (i, 0, j)),
        ],
        out_specs=pl.BlockSpec((None, SUBLANES, tn), lambda i, j: (i, 0, j)),
        compiler_params=_cparams(("parallel", "parallel")),
        name="mods",
    )(c8, mod_w, mod_b.reshape(depth, 1, n))
    return out[:, :bsz].reshape(depth, bsz * 3, 3, d)


def _norm_mod_kernel(x_ref, w_ref, m_ref, o_ref, r_ref):
    n_rows = x_ref.shape[0]
    _inv_rms_rows(x_ref, r_ref, n_rows)
    wmod = w_ref[...] * (1.0 + m_ref[1:2, :])
    shift = m_ref[0:1, :]

    def body(rows):
        o_ref[rows, :] = (x_ref[rows, :] * r_ref[rows, :] * wmod + shift).astype(o_ref.dtype)
    _row_chunks(n_rows, ROW_CHUNK, body)


def _norm_mod(x2, pre_w, mods_l, sub, seq):
    t, d = x2.shape
    tm = 256
    per_b = seq // tm
    return pl.pallas_call(
        _norm_mod_kernel,
        out_shape=jax.ShapeDtypeStruct((t, d), BF16),
        grid=(t // tm,),
        in_specs=[
            pl.BlockSpec((tm, d), lambda i: (i, 0)),
            pl.BlockSpec((1, d), lambda i: (0, 0)),
            pl.BlockSpec((None, 3, d), lambda i: ((i // per_b) * 3 + sub, 0, 0)),
        ],
        out_specs=pl.BlockSpec((tm, d), lambda i: (i, 0)),
        scratch_shapes=[pltpu.VMEM((tm, 1), F32)],
        compiler_params=_cparams(("parallel",)),
        name="norm_mod",
    )(x2, pre_w.reshape(1, d), mods_l)


MM_TM = 1024


def _mm_kernel(a_ref, w_ref, o_ref):
    o_ref[...] = _dot_nt(a_ref[...], w_ref[...].astype(BF16)).astype(o_ref.dtype)


def _mm_side_kernel(a_ref, w_ref, ws_ref, o_ref, os_ref):
    _mm_kernel(a_ref, w_ref, o_ref)

    @pl.when(pl.program_id(1) == 0)
    def _():
        os_ref[...] = _dot_nt(a_ref[...], ws_ref[...].astype(BF16))


def _matmul_wt(a, w_t, out_dtype, tm, tn, n_out, lead, row0=0, side_w=None):
    m, k = a.shape
    if row0 % tn == 0:
        w_spec = pl.BlockSpec((None, tn, k), lambda i, j: (lead, row0 // tn + j, 0))
    else:
        w_t = w_t[lead]
        assert row0 % SUBLANES == 0 and tn % SUBLANES == 0
        w_spec = pl.BlockSpec((pl.Element(tn), pl.Element(k)),
                              lambda i, j: ((row0 // SUBLANES + j * (tn // SUBLANES)) * SUBLANES, 0))
    a_spec = pl.BlockSpec((tm, k), lambda i, j: (i, 0))
    o_spec = pl.BlockSpec((tm, tn), lambda i, j: (i, j))
    if side_w is None:
        return pl.pallas_call(
            _mm_kernel,
            out_shape=jax.ShapeDtypeStruct((m, n_out), out_dtype),
            grid=(m // tm, n_out // tn),
            in_specs=[a_spec, w_spec],
            out_specs=o_spec,
            compiler_params=_cparams(("parallel", "parallel")),
            name="matmul",
        )(a, w_t)
    return pl.pallas_call(
        _mm_side_kernel,
        out_shape=(jax.ShapeDtypeStruct((m, n_out), out_dtype), jax.ShapeDtypeStruct((m, LANES), F32)),
        grid=(m // tm, n_out // tn),
        in_specs=[a_spec, w_spec, pl.BlockSpec((LANES, k), lambda i, j: (0, 0))],
        out_specs=(o_spec, pl.BlockSpec((tm, LANES), lambda i, j: (i, 0))),
        compiler_params=_cparams(("parallel", "arbitrary")),
        name="matmul_side",
    )(a, w_t, side_w)


FFN_TM = 1024
FFN_TF = 512
FFN_SUB = 256


def _ffn_kernel(x_ref, m_ref, prew_ref, postw_ref, wg_ref, wu_ref, wd_ref, o_ref, h_ref, r_ref, *, nf, tm, tf):
    f = pl.program_id(1)

    @pl.when(f == 0)
    def _():
        _inv_rms_rows(x_ref, r_ref, tm)
        wmod = prew_ref[...] * (1.0 + m_ref[1:2, :])
        shift = m_ref[0:1, :]

        def pro(rows):
            h_ref[rows, :] = (x_ref[rows, :] * r_ref[rows, :] * wmod + shift).astype(BF16)
            o_ref[rows, :] = jnp.zeros((ROW_CHUNK, o_ref.shape[1]), F32)
        _row_chunks(tm, ROW_CHUNK, pro)

    h = h_ref[...]
    for c in range(tf // FFN_SUB):
        cs = slice(c * FFN_SUB, (c + 1) * FFN_SUB)
        g = _dot(h, wg_ref[:, cs].astype(BF16))
        u = _dot(h, wu_ref[:, cs].astype(BF16))
        a = (_silu(g) * u).astype(BF16)
        o_ref[...] += _dot(a, wd_ref[cs, :].astype(BF16))

    @pl.when(f == nf - 1)
    def _():
        _inv_rms_rows(o_ref, r_ref, tm)
        gmod = postw_ref[...] * (0.5 * m_ref[2:3, :])

        def epi(rows):
            o_ref[rows, :] = x_ref[rows, :] + o_ref[rows, :] * r_ref[rows, :] * gmod
        _row_chunks(tm, ROW_CHUNK, epi)


def _ffn(x2, mods_l, sub, pre_w, post_w, w_gate, w_up, w_down, layer, which, seq):
    t, d = x2.shape
    dff = w_gate.shape[-1]
    tm, tf = min(FFN_TM, seq), FFN_TF
    nf = dff // tf
    per_b = seq // tm
    one = pl.Buffered(1)
    return pl.pallas_call(
        functools.partial(_ffn_kernel, nf=nf, tm=tm, tf=tf),
        out_shape=jax.ShapeDtypeStruct((t, d), F32),
        grid=(t // tm, nf),
        in_specs=[
            pl.BlockSpec((tm, d), lambda i, f: (i, 0), pipeline_mode=one),
            pl.BlockSpec((None, 3, d), lambda i, f: ((i // per_b) * 3 + sub, 0, 0)),
            pl.BlockSpec((1, d), lambda i, f: (0, 0)),
            pl.BlockSpec((1, d), lambda i, f: (0, 0)),
            pl.BlockSpec((None, None, d, tf), lambda i, f: (layer, which, 0, f)),
            pl.BlockSpec((None, None, d, tf), lambda i, f: (layer, which, 0, f)),
            pl.BlockSpec((None, None, tf, d), lambda i, f: (layer, which, f, 0)),
        ],
        out_specs=pl.BlockSpec((tm, d), lambda i, f: (i, 0), pipeline_mode=one),
        scratch_shapes=[pltpu.VMEM((tm, d), BF16), pltpu.VMEM((tm, 1), F32)],
        compiler_params=_cparams(("parallel", "arbitrary")),
        name="ffn",
    )(x2, mods_l, pre_w.reshape(1, d), post_w.reshape(1, d), w_gate, w_up, w_down)


PROJ_TM = 512


def _proj_resid_kernel(*refs, n_src, tm):
    a_refs = refs[:n_src]
    w_ref, x_ref, m_ref, postw_ref, o_ref, r_ref = refs[n_src:]
    ks = a_refs[0].shape[1]
    acc = _dot(a_refs[0][...], w_ref[0:ks, :])
    for s in range(1, n_src):
        acc = acc + _dot(a_refs[s][...], w_ref[s * ks:(s + 1) * ks, :])
    o_ref[...] = acc

    _inv_rms_rows(o_ref, r_ref, tm)
    gmod = postw_ref[...] * m_ref[2:3, :]

    def epi(rows):
        o_ref[rows, :] = x_ref[rows, :] + o_ref[rows, :] * r_ref[rows, :] * gmod
    _row_chunks(tm, ROW_CHUNK, epi)


def _proj_resid(srcs, w_bf, x2, mods_l, sub, post_w, seq):
    t, d = x2.shape
    n_src = len(srcs)
    ks = srcs[0].shape[1]
    tm = min(PROJ_TM, seq)
    per_b = seq // tm
    return pl.pallas_call(
        functools.partial(_proj_resid_kernel, n_src=n_src, tm=tm),
        out_shape=jax.ShapeDtypeStruct((t, d), F32),
        grid=(t // tm,),
        in_specs=[pl.BlockSpec((tm, ks), lambda i: (i, 0)) for _ in range(n_src)] + [
            pl.BlockSpec((n_src * ks, d), lambda i: (0, 0), pipeline_mode=pl.Buffered(1)),
            pl.BlockSpec((tm, d), lambda i: (i, 0)),
            pl.BlockSpec((None, 3, d), lambda i: ((i // per_b) * 3 + sub, 0, 0)),
            pl.BlockSpec((1, d), lambda i: (0, 0)),
        ],
        out_specs=pl.BlockSpec((tm, d), lambda i: (i, 0)),
        scratch_shapes=[pltpu.VMEM((tm, 1), F32)],
        compiler_params=_cparams(("parallel",)),
        name="proj_resid",
    )(*srcs, w_bf, x2, mods_l, post_w.reshape(1, d))


SSD_D = SSD_HEADS * SSD_HEAD_DIM
SSD_GW = SSD_D // SSD_GROUPS
SSD_BCW = SSD_GROUPS * SSD_STATE


def _ssd_kernel(z_ref, xs_ref, bc_ref, xsp_ref, bcp_ref, dt_ref, dtT_ref, cw_ref, cb_ref, alr_ref, alc_ref,
                dbr_ref, dbc_ref, dexp_ref, nw_ref, e_ref, o_ref, s_ref):
    c = pl.program_id(1)
    q = CHUNK

    @pl.when(c == 0)
    def _():
        s_ref[...] = jnp.zeros_like(s_ref)

    n_shift = SSD_CONV - 1
    tt = lax.broadcasted_iota(I32, (n_shift * q, 2 * q), 0)
    rr = lax.broadcasted_iota(I32, (n_shift * q, 2 * q), 1)
    shift_mat = (rr == (tt % q) + q - n_shift + tt // q).astype(BF16)

    def conv(cur_ref, prev_ref, lo, hi):
        cur = cur_ref[...]
        prev = jnp.where(c > 0, prev_ref[...], jnp.zeros_like(cur))
        delayed = _dot(shift_mat, jnp.concatenate([prev, cur], axis=0))
        acc = cb_ref[:, lo:hi] + cw_ref[n_shift:n_shift + 1, lo:hi] * cur.astype(F32)
        for k in range(n_shift):
            acc = acc + cw_ref[k:k + 1, lo:hi] * delayed[k * q:(k + 1) * q]
        return _silu(acc)

    xs = conv(xs_ref, xsp_ref, 0, SSD_D)
    bc = conv(bc_ref, bcp_ref, SSD_D, SSD_D + 2 * SSD_BCW)

    rows = lax.broadcasted_iota(I32, (q, q), 0)
    cols = lax.broadcasted_iota(I32, (q, q), 1)
    causal = rows >= cols
    tril = causal.astype(BF16)
    triu = (rows <= cols).astype(BF16)

    dt = _softplus(dt_ref[:, 0:SSD_HEADS] + dbr_ref[...])
    dA = dt * (-jnp.exp(alr_ref[...]))
    cum = sum(_dot(tril, p) for p in _split3(dA))
    dtT = _softplus(dtT_ref[...] + dbc_ref[...])
    dAT = dtT * (-jnp.exp(alc_ref[...]))
    cumT = sum(_dot(p, triu) for p in _split3(dAT))

    ecum = jnp.exp(cum)
    decs = jnp.exp(cum[q - 1:q, :] - cum)
    stack = jnp.concatenate([dt, ecum, decs], axis=0)
    ex = sum(_dot(p, e_ref[...]) for p in _split3(stack))
    dt_x, ecum_x, decs_x = ex[0:q], ex[q:2 * q], ex[2 * q:3 * q]
    cdec_x = ecum_x[q - 1:q, :]

    xdt = xs * dt_x
    xw = (xdt * decs_x).astype(BF16)
    xdt_b = xdt.astype(BF16)

    y_parts = []
    for g in range(SSD_GROUPS):
        b_g = bc[:, g * SSD_STATE:(g + 1) * SSD_STATE].astype(BF16)
        c_g = bc[:, SSD_BCW + g * SSD_STATE:SSD_BCW + (g + 1) * SSD_STATE].astype(BF16)
        gs = slice(g * SSD_GW, (g + 1) * SSD_GW)
        cb = _dot_nt(c_g, b_g)
        s_g = s_ref[g]
        y_g = _dot(c_g, s_g.astype(BF16)) * ecum_x[:, gs]
        diag = []
        for r in range(SSD_HEADS // SSD_GROUPS):
            hh = g * (SSD_HEADS // SSD_GROUPS) + r
            seg = cum[:, hh:hh + 1] - cumT[hh:hh + 1, :]
            dec = jnp.where(causal, jnp.exp(seg), 0.0)
            m = (cb * dec).astype(BF16)
            hs = slice(g * SSD_GW + r * SSD_HEAD_DIM, g * SSD_GW + (r + 1) * SSD_HEAD_DIM)
            diag.append(_dot(m, xdt_b[:, hs]))
        y_parts.append(y_g + jnp.concatenate(diag, axis=1))
        s_ref[g] = s_g * cdec_x[:, gs] + _dot_tn(b_g, xw[:, gs])
    y = jnp.concatenate(y_parts, axis=1) + xs * dexp_ref[...]
    o_ref[...] = _rms(y * _silu(z_ref[...].astype(F32)), nw_ref[...]).astype(o_ref.dtype)


def _ssd(proj0, dt_pad, dtT, conv_w, conv_b, a_log, dt_bias, d_skip, norm_w, bsz, seq):
    t = proj0.shape[0]
    nc = seq // CHUNK
    h = SSD_HEADS
    expand = jnp.repeat(jnp.eye(h, dtype=BF16), SSD_HEAD_DIM, axis=1)
    dexp = jnp.repeat(d_skip.astype(F32), SSD_HEAD_DIM).reshape(1, SSD_D)
    full = lambda shape: pl.BlockSpec(shape, lambda b, c: (0,) * len(shape))
    col = lambda j: pl.BlockSpec((CHUNK, SSD_D), lambda b, c: (b * nc + c, j))
    prev = lambda j: pl.BlockSpec((CHUNK, SSD_D), lambda b, c: (b * nc + jnp.maximum(c - 1, 0), j))
    return pl.pallas_call(
        _ssd_kernel,
        out_shape=jax.ShapeDtypeStruct((t, SSD_D), BF16),
        grid=(bsz, nc),
        in_specs=[
            col(0), col(1), col(2), prev(1), prev(2),
            pl.BlockSpec((CHUNK, LANES), lambda b, c: (b * nc + c, 0)),
            pl.BlockSpec((None, h, CHUNK), lambda b, c: (b, 0, c)),
            full((SSD_CONV, SSD_D + 2 * SSD_BCW)), full((1, SSD_D + 2 * SSD_BCW)),
            full((1, h)), full((h, 1)), full((1, h)), full((h, 1)),
            full((1, SSD_D)), full((1, SSD_D)), full((h, SSD_D)),
        ],
        out_specs=pl.BlockSpec((CHUNK, SSD_D), lambda b, c: (b * nc + c, 0)),
        scratch_shapes=[pltpu.VMEM((SSD_GROUPS, SSD_STATE, SSD_GW), F32)],
        compiler_params=_cparams(("parallel", "arbitrary")),
        name="ssd",
    )(proj0, proj0, proj0, proj0, proj0, dt_pad, dtT, conv_w, conv_b.reshape(1, -1),
      a_log.reshape(1, h), a_log.reshape(h, 1), dt_bias.reshape(1, h), dt_bias.reshape(h, 1),
      dexp, norm_w.reshape(1, SSD_D), expand)


RET_D = RET_HEADS * RET_DIM


def _ret_kernel(q_ref, k_ref, v_ref, g_ref, pos_ref, inv_ref, nw_ref, o_ref, r_ref):
    c = pl.program_id(1)
    q = CHUNK
    half = RET_DIM // 2

    @pl.when(c == 0)
    def _():
        r_ref[...] = jnp.zeros_like(r_ref)

    ang = pos_ref[...].astype(F32) * inv_ref[...]
    cs, sn = jnp.cos(ang), jnp.sin(ang)
    li = lax.broadcasted_iota(I32, (q, q), 0)
    si = lax.broadcasted_iota(I32, (q, q), 1)
    dist = (li - si).astype(F32)
    lcol = lax.broadcasted_iota(I32, (q, 1), 0).astype(F32)

    def rope(x):
        x1, x2 = x[:, :half], x[:, half:]
        return jnp.concatenate([x1 * cs - x2 * sn, x1 * sn + x2 * cs], axis=1)

    for h in range(RET_HEADS):
        lg = math.log(1.0 - 2.0 ** (-5.0 - h))
        hs = slice(h * RET_DIM, (h + 1) * RET_DIM)
        qr = rope(q_ref[:, hs].astype(F32)).astype(BF16)
        kr = rope(k_ref[:, hs].astype(F32)) * (RET_DIM ** -0.5)
        v = v_ref[:, hs]
        intra = jnp.where(dist >= 0.0, jnp.exp(dist * lg), 0.0)
        sc = _dot_nt(qr, kr.astype(BF16)) * intra
        r_h = r_ref[h]
        y = _dot(sc.astype(BF16), v) + _dot(qr, r_h.astype(BF16)) * jnp.exp((lcol + 1.0) * lg)
        kd = (kr * jnp.exp((q - 1.0 - lcol) * lg)).astype(BF16)
        r_ref[h] = r_h * math.exp(q * lg) + _dot_tn(kd, v)
        gate = _silu(g_ref[:, hs].astype(F32))
        o_ref[:, hs] = (gate * _rms(y, nw_ref[:, hs])).astype(o_ref.dtype)


def _retention(proj0, pos_col, norm_w, bsz, seq):
    t = proj0.shape[0]
    nc = seq // CHUNK
    half = RET_DIM // 2
    inv = (RET_THETA ** (-jnp.arange(half, dtype=F32) / half)).reshape(1, half)
    col = lambda j: pl.BlockSpec((CHUNK, RET_D), lambda b, c: (b * nc + c, j))
    return pl.pallas_call(
        _ret_kernel,
        out_shape=jax.ShapeDtypeStruct((t, RET_D), BF16),
        grid=(bsz, nc),
        in_specs=[
            col(0), col(1), col(2), col(3),
            pl.BlockSpec((CHUNK, 1), lambda b, c: (b * nc + c, 0)),
            pl.BlockSpec((1, half), lambda b, c: (0, 0)),
            pl.BlockSpec((1, RET_D), lambda b, c: (0, 0)),
        ],
        out_specs=pl.BlockSpec((CHUNK, RET_D), lambda b, c: (b * nc + c, 0)),
        scratch_shapes=[pltpu.VMEM((RET_HEADS, RET_DIM, RET_DIM), F32)],
        compiler_params=_cparams(("parallel", "arbitrary")),
        name="retention",
    )(proj0, proj0, proj0, proj0, pos_col, inv, norm_w.reshape(1, RET_D))


ATT_D = ATT_HEADS * ATT_DIM
IDX_D = IDX_HEADS * IDX_DIM
ATT_ROT = ATT_DIM // ROPE_FRACTION
IDX_ROT = IDX_DIM // ROPE_FRACTION
PREP_TM = 256
VT_ROWS = ATT_DIM + 16
Q_SCALE = ATT_DIM ** -0.5 * math.log2(math.e)


def _rope_factors(ang, half, period):
    lane = lax.broadcasted_iota(I32, ang.shape, 1) % period
    first = lane < half
    second = jnp.logical_and(lane >= half, lane < 2 * half)
    cs, sn = jnp.cos(ang), jnp.sin(ang)
    cfac = jnp.where(jnp.logical_or(first, second), cs, 1.0)
    sfac = jnp.where(first, -sn, jnp.where(second, sn, 0.0))
    return first, cfac, sfac, half


def _partial_rope(x, factors):
    first, cfac, sfac, half = factors
    partner = jnp.where(first, pltpu.roll(x, LANES - half, axis=1), pltpu.roll(x, half, axis=1))
    return x * cfac + partner * sfac


def _dsa_prep_kernel(p_ref, tail_ref, pos_ref, inva_ref, invi_ref, knw_ref,
                     qh_ref, qih_ref, k_ref, vT_ref, ki_ref, wi_ref):
    pos = pos_ref[...].astype(F32)
    rope_a = _rope_factors(pos * inva_ref[...], ATT_ROT // 2, ATT_DIM)
    rope_i = _rope_factors(pos * invi_ref[...], IDX_ROT // 2, IDX_DIM)
    first_a, cfac_a, sfac_a, half_a = rope_a
    rope_q = (first_a, cfac_a * Q_SCALE, sfac_a * Q_SCALE, half_a)
    o_k, o_v, o_qi = ATT_D, ATT_D + ATT_DIM, ATT_D + 2 * ATT_DIM

    for h in range(ATT_HEADS):
        x = p_ref[:, h * ATT_DIM:(h + 1) * ATT_DIM].astype(F32)
        qh_ref[h] = _partial_rope(x, rope_q).astype(BF16)
    for p in range(IDX_HEADS // 2):
        x = p_ref[:, o_qi + p * LANES:o_qi + (p + 1) * LANES].astype(F32)
        y = _partial_rope(x, rope_i).astype(BF16)
        qih_ref[2 * p] = y[:, :IDX_DIM]
        qih_ref[2 * p + 1] = y[:, IDX_DIM:]

    k = p_ref[:, o_k:o_v].astype(F32)
    k_ref[...] = _partial_rope(k, rope_a).astype(BF16)
    vT_ref[0:ATT_DIM, :] = jnp.transpose(p_ref[:, o_v:o_qi].astype(F32)).astype(BF16)
    ones_row = lax.broadcasted_iota(I32, (VT_ROWS - ATT_DIM, vT_ref.shape[1]), 0) == 0
    vT_ref[ATT_DIM:VT_ROWS, :] = jnp.where(ones_row, 1.0, 0.0).astype(BF16)

    tail = tail_ref[...]
    lane = lax.broadcasted_iota(I32, tail.shape, 1)
    ki = jnp.where(lane < IDX_DIM, tail, 0.0)
    ms = jnp.sum(ki * ki, axis=-1, keepdims=True) * (1.0 / IDX_DIM)
    kin = ki * lax.rsqrt(ms + EPS) * knw_ref[...]
    ki_ref[...] = _partial_rope(kin, rope_i)[:, :IDX_DIM].astype(BF16)
    wi_ref[...] = tail[:, IDX_DIM:IDX_DIM + IDX_HEADS] * (IDX_HEADS ** -0.5 * IDX_DIM ** -0.5)


def _dsa_prep(proj1, tail, pos_col, knw, bsz, seq):
    t = proj1.shape[0]
    tm = PREP_TM
    per_b = seq // tm
    ha, hi = ATT_ROT // 2, IDX_ROT // 2
    inv_a = ROPE_THETA ** (-jnp.arange(ha, dtype=F32) / ha)
    inv_i = ROPE_THETA ** (-jnp.arange(hi, dtype=F32) / hi)
    lane = jnp.arange(LANES)
    inva = jnp.where(lane < ATT_ROT, inv_a[lane % ha], 0.0).reshape(1, LANES)
    invi = jnp.where(lane % IDX_DIM < IDX_ROT, inv_i[lane % hi], 0.0).reshape(1, LANES)
    knw_pad = jnp.zeros((1, LANES), F32).at[0, :IDX_DIM].set(knw)
    return pl.pallas_call(
        _dsa_prep_kernel,
        out_shape=(
            jax.ShapeDtypeStruct((ATT_HEADS, t, ATT_DIM), BF16),
            jax.ShapeDtypeStruct((IDX_HEADS, t, IDX_DIM), BF16),
            jax.ShapeDtypeStruct((t, ATT_DIM), BF16),
            jax.ShapeDtypeStruct((bsz, VT_ROWS, seq), BF16),
            jax.ShapeDtypeStruct((t, IDX_DIM), BF16),
            jax.ShapeDtypeStruct((t, IDX_HEADS), F32),
        ),
        grid=(t // tm,),
        in_specs=[
            pl.BlockSpec((tm, proj1.shape[1]), lambda i: (i, 0)),
            pl.BlockSpec((tm, LANES), lambda i: (i, 0)),
            pl.BlockSpec((tm, 1), lambda i: (i, 0)),
            pl.BlockSpec((1, LANES), lambda i: (0, 0)),
            pl.BlockSpec((1, LANES), lambda i: (0, 0)),
            pl.BlockSpec((1, LANES), lambda i: (0, 0)),
        ],
        out_specs=(
            pl.BlockSpec((ATT_HEADS, tm, ATT_DIM), lambda i: (0, i, 0)),
            pl.BlockSpec((IDX_HEADS, tm, IDX_DIM), lambda i: (0, i, 0)),
            pl.BlockSpec((tm, ATT_DIM), lambda i: (i, 0)),
            pl.BlockSpec((None, VT_ROWS, tm), lambda i: (i // per_b, 0, i % per_b)),
            pl.BlockSpec((tm, IDX_DIM), lambda i: (i, 0)),
            pl.BlockSpec((tm, IDX_HEADS), lambda i: (i, 0)),
        ),
        compiler_params=_cparams(("parallel",)),
        name="dsa_prep",
    )(proj1, tail, pos_col, inva, invi, knw_pad)


DSA_KGROUP = 512
HEAD_PAIR = 2


REDUCE_SLABS = 8


def _col_reduce(x, reduce_fn, combine_fn):
    n = x.shape[0] // REDUCE_SLABS
    parts = [reduce_fn(x[g * n:(g + 1) * n], axis=0, keepdims=True) for g in range(REDUCE_SLABS)]
    while len(parts) > 1:
        parts = [combine_fn(parts[i], parts[i + 1]) for i in range(0, len(parts), 2)]
    return parts[0]


def _col_sum(x):
    return _col_reduce(x, jnp.sum, jnp.add)


def _col_max(x):
    return _col_reduce(x, jnp.max, jnp.maximum)


def _key_to_float(key):
    return pltpu.bitcast(jnp.where(key < 0, key ^ INT_MAX, key), F32)


def _dsa_body(lk, topk, j, qh_ref, qih_ref, wiT_ref, k_ref, vT_ref, ki_ref, o_ref):
    qb = Q_BLOCK
    kidx = lax.broadcasted_iota(I32, (lk, qb), 0)
    qpos = j * qb + lax.broadcasted_iota(I32, (lk, qb), 1)
    visible = kidx <= qpos

    ki = ki_ref[0:lk, :]
    score = jnp.zeros((lk, qb), F32)
    for h in range(0, IDX_HEADS, HEAD_PAIR):
        qi = qih_ref[h:h + HEAD_PAIR].reshape(HEAD_PAIR * qb, IDX_DIM)
        s = jnp.maximum(_dot_nt(ki, qi), 0.0)
        for r in range(HEAD_PAIR):
            score = score + s[:, r * qb:(r + 1) * qb] * wiT_ref[h + r:h + r + 1, :]
    score = jnp.where(visible, score, -jnp.inf)

    def count(mask):
        return _col_sum(jnp.where(mask, 1, 0).astype(I32))

    def bit_step(i, ans):
        trial = jnp.where(i == 0, jnp.zeros_like(ans), ans | jnp.left_shift(1, 31 - i))
        return jnp.where(count(score >= _key_to_float(trial)) >= topk, trial, ans)

    thr = _key_to_float(lax.fori_loop(0, 32, bit_step, jnp.full((1, qb), INT_MIN, I32)))
    above = score > thr
    equal = score == thr
    need = topk - count(above)
    over = count(equal) > need
    few = (j * qb + lax.broadcasted_iota(I32, (1, qb), 1)) < topk

    def tie_cut():
        nbits = max(lk - 1, 1).bit_length()

        def idx_step(i, ans):
            trial = ans | jnp.left_shift(1, nbits - 1 - i)
            c = count(jnp.logical_and(equal, kidx < trial))
            return jnp.where(c < need, trial, ans)

        cut = lax.fori_loop(0, nbits, idx_step, jnp.zeros((1, qb), I32))
        return jnp.where(over, cut, INT_MAX)

    cut = lax.cond(jnp.max(over.astype(I32)) > 0, tie_cut, lambda: jnp.full((1, qb), INT_MAX, I32))
    picked = jnp.logical_or(above, jnp.logical_and(equal, kidx <= cut))
    sel = jnp.logical_and(visible, jnp.logical_or(few, picked))
    bias = jnp.where(sel, 0.0, NEG_BIG)

    k = k_ref[0:lk, :]
    vT = vT_ref[:, 0:lk]
    bias2 = jnp.concatenate([bias] * HEAD_PAIR, axis=1)
    for h in range(0, ATT_HEADS, HEAD_PAIR):
        qq = qh_ref[h:h + HEAD_PAIR].reshape(HEAD_PAIR * qb, ATT_DIM)
        s = _dot_nt(k, qq) + bias2
        m = _col_max(s)
        p = jnp.exp2(s - m).astype(BF16)
        ov = _dot(vT, p)
        oT = ov[0:ATT_DIM] / ov[ATT_DIM:ATT_DIM + 1]
        for r in range(HEAD_PAIR):
            o_ref[:, (h + r) * ATT_DIM:(h + r + 1) * ATT_DIM] = (
                jnp.transpose(oT[:, r * qb:(r + 1) * qb]).astype(o_ref.dtype))


def _dsa_kernel(qh_ref, qih_ref, wiT_ref, k_ref, vT_ref, ki_ref, o_ref, *, seq):
    j = pl.program_id(1)
    per_group = DSA_KGROUP // Q_BLOCK
    topk = min(TOPK, seq // 4)
    for g in range(seq // DSA_KGROUP):
        @pl.when(j // per_group == g)
        def _(g=g):
            _dsa_body((g + 1) * DSA_KGROUP, topk, j, qh_ref, qih_ref, wiT_ref, k_ref, vT_ref, ki_ref, o_ref)


def _dsa_attention(qh, qih, wiT, k_r, vT, ki_r, bsz, seq):
    t = k_r.shape[0]
    nb = seq // Q_BLOCK
    return pl.pallas_call(
        functools.partial(_dsa_kernel, seq=seq),
        out_shape=jax.ShapeDtypeStruct((t, ATT_D), BF16),
        grid=(bsz, nb),
        in_specs=[
            pl.BlockSpec((ATT_HEADS, Q_BLOCK, ATT_DIM), lambda b, j: (0, b * nb + j, 0)),
            pl.BlockSpec((IDX_HEADS, Q_BLOCK, IDX_DIM), lambda b, j: (0, b * nb + j, 0)),
            pl.BlockSpec((IDX_HEADS, Q_BLOCK), lambda b, j: (0, b * nb + j)),
            pl.BlockSpec((seq, ATT_DIM), lambda b, j: (b, 0)),
            pl.BlockSpec((None, VT_ROWS, seq), lambda b, j: (b, 0, 0)),
            pl.BlockSpec((seq, IDX_DIM), lambda b, j: (b, 0)),
        ],
        out_specs=pl.BlockSpec((Q_BLOCK, ATT_D), lambda b, j: (b * nb + j, 0)),
        compiler_params=_cparams(("parallel", "parallel")),
        name="dsa_attention",
    )(qh, qih, wiT, k_r, vT, ki_r)


def _pad_rows(w, n):
    return jnp.pad(w, ((0, n - w.shape[0]), (0, 0)))


def _hybrid_mixer(x2, mods_l, norm_w_l, pos_col, w_in, e, conv_w, conv_b, a_log, dt_bias, d_skip, ssd_norm_w,
                  ret_norm_w, w_out, bsz, seq):
    h = _norm_mod(x2, norm_w_l[2], mods_l, 1, seq)
    tm = min(MM_TM, seq)
    n_zx = SSD_D + SSD_D + 2 * SSD_BCW
    ret_lo = n_zx + SSD_HEADS
    w_t = jnp.swapaxes(w_in, 1, 2)
    w_dt = _pad_rows(w_t[e, n_zx:ret_lo], LANES)
    proj_zx, dt_pad = _matmul_wt(h, w_t, BF16, tm, 1024, n_zx, e, side_w=w_dt)
    proj_ret = _matmul_wt(h, w_t, BF16, tm, 1024, w_t.shape[1] - ret_lo, e, row0=ret_lo)
    dtT = jnp.swapaxes(dt_pad[:, :SSD_HEADS].reshape(bsz, seq, SSD_HEADS), 1, 2)
    ya = _ssd(proj_zx, dt_pad, dtT, conv_w, conv_b, a_log, dt_bias, d_skip, ssd_norm_w, bsz, seq)
    yb = _retention(proj_ret, pos_col, ret_norm_w, bsz, seq)
    return _proj_resid([ya, yb], w_out[e].astype(BF16), x2, mods_l, 1, norm_w_l[3], seq)


def _dsa_mixer(x2, mods_l, norm_w_l, pos_col, w_in, o, knw, w_out, bsz, seq):
    h = _norm_mod(x2, norm_w_l[2], mods_l, 1, seq)
    tm = min(MM_TM, seq)
    n_main = ATT_D + 2 * ATT_DIM + IDX_D
    w_t = jnp.swapaxes(w_in, 1, 2)
    w_tail = _pad_rows(w_t[o, n_main:], LANES)
    proj1, tail = _matmul_wt(h, w_t, BF16, tm, n_main // 2, n_main, o, side_w=w_tail)
    qh, qih, k_r, vT, ki_r, wi = _dsa_prep(proj1, tail, pos_col, knw, bsz, seq)
    att = _dsa_attention(qh, qih, wi.T, k_r, vT, ki_r, bsz, seq)
    return _proj_resid([att], w_out[o].astype(BF16), x2, mods_l, 1, norm_w_l[3], seq)


def kernel(x, c, positions, mod_w, mod_b, norm_w, ffn_w_gate, ffn_w_up, ffn_w_down, hy_w_in, hy_conv_w,
           hy_conv_b, ssd_A_log, ssd_dt_bias, ssd_D, ssd_norm_w, ret_norm_w, hy_w_out, dsa_w_in,
           idx_k_norm_w, dsa_w_out):
    bsz, seq, d = x.shape
    depth = mod_w.shape[0]
    x2 = x.reshape(bsz * seq, d)
    pos_col = positions.reshape(bsz * seq, 1)
    mods = _mods(c, mod_w, mod_b)
    for i in range(depth):
        ml, nw = mods[i], norm_w[i]
        x2 = _ffn(x2, ml, 0, nw[0], nw[1], ffn_w_gate, ffn_w_up, ffn_w_down, i, 0, seq)
        if i % 2 == 0:
            e = i // 2
            x2 = _hybrid_mixer(x2, ml, nw, pos_col, hy_w_in, e, hy_conv_w[e], hy_conv_b[e], ssd_A_log[e],
                               ssd_dt_bias[e], ssd_D[e], ssd_norm_w[e], ret_norm_w[e], hy_w_out, bsz, seq)
        else:
            o = i // 2
            x2 = _dsa_mixer(x2, ml, nw, pos_col, dsa_w_in, o, idx_k_norm_w[o], dsa_w_out, bsz, seq)
        x2 = _ffn(x2, ml, 2, nw[4], nw[5], ffn_w_gate, ffn_w_up, ffn_w_down, i, 1, seq)
    return x2.reshape(bsz, seq, d)
```

```python
import functools
import math

import jax
import jax.numpy as jnp
from jax import lax
from jax.experimental import pallas as pl
from jax.experimental.pallas import tpu as pltpu

F32 = jnp.float32
BF16 = jnp.bfloat16
I32 = jnp.int32

EPS = 1e-6
SSD_HEADS = 32
SSD_HEAD_DIM = 64
SSD_GROUPS = 8
SSD_STATE = 128
SSD_CONV = 4
CHUNK = 128
RET_HEADS = 8
RET_DIM = 256
RET_THETA = 10000.0
ATT_HEADS = 16
ATT_DIM = 128
IDX_HEADS = 16
IDX_DIM = 64
TOPK = 256
Q_BLOCK = 128
ROPE_THETA = 500000.0
ROPE_FRACTION = 4

V7X_VMEM_BYTES = 64 * 1024 * 1024
VMEM_LIMIT = V7X_VMEM_BYTES - 8 * 1024 * 1024
LANES = 128
SUBLANES = 8

INT_MIN = -(2 ** 31)
INT_MAX = 2 ** 31 - 1
NEG_BIG = -1e30


def _cparams(sem):
    return pltpu.CompilerParams(dimension_semantics=sem, vmem_limit_bytes=VMEM_LIMIT)


def _sigmoid(x):
    return 1.0 / (1.0 + jnp.exp(-x))


def _silu(x):
    return x * _sigmoid(x)


def _softplus(x):
    return jnp.maximum(x, 0.0) + jnp.log1p(jnp.exp(-jnp.abs(x)))


def _rms(x, w):
    return x * lax.rsqrt(jnp.mean(x * x, axis=-1, keepdims=True) + EPS) * w


def _dot(a, b):
    return jnp.dot(a, b, preferred_element_type=F32)


def _dot_nt(a, b):
    return lax.dot_general(a, b, (((1,), (1,)), ((), ())), preferred_element_type=F32)


def _dot_tn(a, b):
    return lax.dot_general(a, b, (((0,), (0,)), ((), ())), preferred_element_type=F32)


def _split3(x):
    hi = x.astype(BF16)
    r = x - hi.astype(F32)
    mid = r.astype(BF16)
    lo = (r - mid.astype(F32)).astype(BF16)
    return hi, mid, lo


def _row_chunks(n_rows, rc, fn):
    def body(i, carry):
        fn(pl.ds(pl.multiple_of(i * rc, rc), rc))
        return carry
    lax.fori_loop(0, n_rows // rc, body, 0)


ROW_CHUNK = 128


def _inv_rms_rows(src_ref, r_ref, n_rows):
    def body(rows):
        x = src_ref[rows, :]
        r_ref[rows, :] = lax.rsqrt(jnp.mean(x * x, axis=-1, keepdims=True) + EPS)
    _row_chunks(n_rows, ROW_CHUNK, body)


def _mods_kernel(c_ref, w_ref, b_ref, o_ref):
    cond = _silu(c_ref[...]).astype(BF16)
    o_ref[...] = _dot(cond, w_ref[...].astype(BF16)) + b_ref[...]


def _mods(c, mod_w, mod_b):
    depth, d, n = mod_w.shape
    bsz = c.shape[0]
    c8 = jnp.zeros((SUBLANES, d), F32).at[:bsz].set(c)
    tn = 1024
    out = pl.pallas_call(
        _mods_kernel,
        out_shape=jax.ShapeDtypeStruct((depth, SUBLANES, n), F32),
        grid=(depth, n // tn),
        in_specs=[
            pl.BlockSpec((SUBLANES, d), lambda i, j: (0, 0)),
            pl.BlockSpec((None, d, tn), lambda i, j: (i, 0, j)),
            pl.BlockSpec((None, 1, tn), lambda i, j: (i, 0, j)),
        ],
        out_specs=pl.BlockSpec((None, SUBLANES, tn), lambda i, j: (i, 0, j)),
        compiler_params=_cparams(("parallel", "parallel")),
        name="mods",
    )(c8, mod_w, mod_b.reshape(depth, 1, n))
    return out[:, :bsz].reshape(depth, bsz * 3, 3, d)


NORM_TM = 512


def _norm_mod_kernel(x_ref, w_ref, m_ref, o_ref, r_ref):
    n_rows = x_ref.shape[0]
    _inv_rms_rows(x_ref, r_ref, n_rows)
    wmod = w_ref[...] * (1.0 + m_ref[1:2, :])
    shift = m_ref[0:1, :]

    def body(rows):
        o_ref[rows, :] = (x_ref[rows, :] * r_ref[rows, :] * wmod + shift).astype(o_ref.dtype)
    _row_chunks(n_rows, ROW_CHUNK, body)


def _norm_mod(x2, pre_w, mods_l, sub, seq):
    t, d = x2.shape
    tm = min(NORM_TM, seq)
    per_b = seq // tm
    return pl.pallas_call(
        _norm_mod_kernel,
        out_shape=jax.ShapeDtypeStruct((t, d), BF16),
        grid=(t // tm,),
        in_specs=[
            pl.BlockSpec((tm, d), lambda i: (i, 0)),
            pl.BlockSpec((1, d), lambda i: (0, 0)),
            pl.BlockSpec((None, 3, d), lambda i: ((i // per_b) * 3 + sub, 0, 0)),
        ],
        out_specs=pl.BlockSpec((tm, d), lambda i: (i, 0)),
        scratch_shapes=[pltpu.VMEM((tm, 1), F32)],
        compiler_params=_cparams(("parallel",)),
        name="norm_mod",
    )(x2, pre_w.reshape(1, d), mods_l)


MM_TM = 1024


def _mm_kernel(a_ref, w_ref, o_ref):
    o_ref[...] = _dot_nt(a_ref[...], w_ref[...].astype(BF16)).astype(o_ref.dtype)


def _mm_side_kernel(a_ref, w_ref, ws_ref, o_ref, os_ref):
    _mm_kernel(a_ref, w_ref, o_ref)

    @pl.when(pl.program_id(1) == 0)
    def _():
        os_ref[...] = _dot_nt(a_ref[...], ws_ref[...].astype(BF16))


def _matmul_wt(a, w_t, out_dtype, tm, tn, n_out, lead, row0=0, side_w=None):
    m, k = a.shape
    if row0 % tn == 0:
        w_spec = pl.BlockSpec((None, tn, k), lambda i, j: (lead, row0 // tn + j, 0))
    else:
        w_t = w_t[lead]
        assert row0 % SUBLANES == 0 and tn % SUBLANES == 0
        w_spec = pl.BlockSpec((pl.Element(tn), pl.Element(k)),
                              lambda i, j: ((row0 // SUBLANES + j * (tn // SUBLANES)) * SUBLANES, 0))
    a_spec = pl.BlockSpec((tm, k), lambda i, j: (i, 0))
    o_spec = pl.BlockSpec((tm, tn), lambda i, j: (i, j))
    if side_w is None:
        return pl.pallas_call(
            _mm_kernel,
            out_shape=jax.ShapeDtypeStruct((m, n_out), out_dtype),
            grid=(m // tm, n_out // tn),
            in_specs=[a_spec, w_spec],
            out_specs=o_spec,
            compiler_params=_cparams(("parallel", "parallel")),
            name="matmul",
        )(a, w_t)
    return pl.pallas_call(
        _mm_side_kernel,
        out_shape=(jax.ShapeDtypeStruct((m, n_out), out_dtype), jax.ShapeDtypeStruct((m, LANES), F32)),
        grid=(m // tm, n_out // tn),
        in_specs=[a_spec, w_spec, pl.BlockSpec((LANES, k), lambda i, j: (0, 0))],
        out_specs=(o_spec, pl.BlockSpec((tm, LANES), lambda i, j: (i, 0))),
        compiler_params=_cparams(("parallel", "arbitrary")),
        name="matmul_side",
    )(a, w_t, side_w)


FFN_TM = 1024
FFN_TF = 512
FFN_SUB = 256


def _ffn_kernel(x_ref, m_ref, prew_ref, postw_ref, wg_ref, wu_ref, wd_ref, o_ref, h_ref, r_ref, *, nf, tm, tf):
    f = pl.program_id(1)

    @pl.when(f == 0)
    def _():
        _inv_rms_rows(x_ref, r_ref, tm)
        wmod = prew_ref[...] * (1.0 + m_ref[1:2, :])
        shift = m_ref[0:1, :]

        def pro(rows):
            h_ref[rows, :] = (x_ref[rows, :] * r_ref[rows, :] * wmod + shift).astype(BF16)
            o_ref[rows, :] = jnp.zeros((ROW_CHUNK, o_ref.shape[1]), F32)
        _row_chunks(tm, ROW_CHUNK, pro)

    h = h_ref[...]
    acts = []
    for c in range(tf // FFN_SUB):
        cs = slice(c * FFN_SUB, (c + 1) * FFN_SUB)
        g = _dot(h, wg_ref[:, cs].astype(BF16))
        u = _dot(h, wu_ref[:, cs].astype(BF16))
        acts.append((_silu(g) * u).astype(BF16))
    o_ref[...] += _dot(jnp.concatenate(acts, axis=1), wd_ref[...].astype(BF16))

    @pl.when(f == nf - 1)
    def _():
        _inv_rms_rows(o_ref, r_ref, tm)
        gmod = postw_ref[...] * (0.5 * m_ref[2:3, :])

        def epi(rows):
            o_ref[rows, :] = x_ref[rows, :] + o_ref[rows, :] * r_ref[rows, :] * gmod
        _row_chunks(tm, ROW_CHUNK, epi)


def _ffn(x2, mods_l, sub, pre_w, post_w, w_gate, w_up, w_down, layer, which, seq):
    t, d = x2.shape
    dff = w_gate.shape[-1]
    tm, tf = min(FFN_TM, seq), FFN_TF
    nf = dff // tf
    per_b = seq // tm
    one = pl.Buffered(1)
    return pl.pallas_call(
        functools.partial(_ffn_kernel, nf=nf, tm=tm, tf=tf),
        out_shape=jax.ShapeDtypeStruct((t, d), F32),
        grid=(t // tm, nf),
        in_specs=[
            pl.BlockSpec((tm, d), lambda i, f: (i, 0), pipeline_mode=one),
            pl.BlockSpec((None, 3, d), lambda i, f: ((i // per_b) * 3 + sub, 0, 0)),
            pl.BlockSpec((1, d), lambda i, f: (0, 0)),
            pl.BlockSpec((1, d), lambda i, f: (0, 0)),
            pl.BlockSpec((None, None, d, tf), lambda i, f: (layer, which, 0, f)),
            pl.BlockSpec((None, None, d, tf), lambda i, f: (layer, which, 0, f)),
            pl.BlockSpec((None, None, tf, d), lambda i, f: (layer, which, f, 0)),
        ],
        out_specs=pl.BlockSpec((tm, d), lambda i, f: (i, 0), pipeline_mode=one),
        scratch_shapes=[pltpu.VMEM((tm, d), BF16), pltpu.VMEM((tm, 1), F32)],
        compiler_params=_cparams(("parallel", "arbitrary")),
        name="ffn",
    )(x2, mods_l, pre_w.reshape(1, d), post_w.reshape(1, d), w_gate, w_up, w_down)


PROJ_TM = 512


def _proj_resid_kernel(*refs, n_src, tm):
    a_refs = refs[:n_src]
    w_ref, x_ref, m_ref, postw_ref, o_ref, r_ref = refs[n_src:]
    ks = a_refs[0].shape[1]
    acc = _dot(a_refs[0][...], w_ref[0:ks, :])
    for s in range(1, n_src):
        acc = acc + _dot(a_refs[s][...], w_ref[s * ks:(s + 1) * ks, :])
    o_ref[...] = acc

    _inv_rms_rows(o_ref, r_ref, tm)
    gmod = postw_ref[...] * m_ref[2:3, :]

    def epi(rows):
        o_ref[rows, :] = x_ref[rows, :] + o_ref[rows, :] * r_ref[rows, :] * gmod
    _row_chunks(tm, ROW_CHUNK, epi)


def _proj_resid(srcs, w_bf, x2, mods_l, sub, post_w, seq):
    t, d = x2.shape
    n_src = len(srcs)
    ks = srcs[0].shape[1]
    tm = min(PROJ_TM, seq)
    per_b = seq // tm
    return pl.pallas_call(
        functools.partial(_proj_resid_kernel, n_src=n_src, tm=tm),
        out_shape=jax.ShapeDtypeStruct((t, d), F32),
        grid=(t // tm,),
        in_specs=[pl.BlockSpec((tm, ks), lambda i: (i, 0)) for _ in range(n_src)] + [
            pl.BlockSpec((n_src * ks, d), lambda i: (0, 0), pipeline_mode=pl.Buffered(1)),
            pl.BlockSpec((tm, d), lambda i: (i, 0)),
            pl.BlockSpec((None, 3, d), lambda i: ((i // per_b) * 3 + sub, 0, 0)),
            pl.BlockSpec((1, d), lambda i: (0, 0)),
        ],
        out_specs=pl.BlockSpec((tm, d), lambda i: (i, 0)),
        scratch_shapes=[pltpu.VMEM((tm, 1), F32)],
        compiler_params=_cparams(("parallel",)),
        name="proj_resid",
    )(*srcs, w_bf, x2, mods_l, post_w.reshape(1, d))


SSD_D = SSD_HEADS * SSD_HEAD_DIM
SSD_GW = SSD_D // SSD_GROUPS
SSD_BCW = SSD_GROUPS * SSD_STATE


def _ssd_kernel(z_ref, xs_ref, bc_ref, xsp_ref, bcp_ref, dt_ref, dtT_ref, cw_ref, cb_ref, alr_ref, alc_ref,
                dbr_ref, dbc_ref, dexp_ref, nw_ref, e_ref, o_ref, s_ref):
    c = pl.program_id(1)
    q = CHUNK

    @pl.when(c == 0)
    def _():
        s_ref[...] = jnp.zeros_like(s_ref)

    n_shift = SSD_CONV - 1
    tt = lax.broadcasted_iota(I32, (n_shift * q, 2 * q), 0)
    rr = lax.broadcasted_iota(I32, (n_shift * q, 2 * q), 1)
    shift_mat = (rr == (tt % q) + q - n_shift + tt // q).astype(BF16)

    def conv(cur_ref, prev_ref, lo, hi):
        cur = cur_ref[...]
        prev = jnp.where(c > 0, prev_ref[...], jnp.zeros_like(cur))
        delayed = _dot(shift_mat, jnp.concatenate([prev, cur], axis=0))
        acc = cb_ref[:, lo:hi] + cw_ref[n_shift:n_shift + 1, lo:hi] * cur.astype(F32)
        for k in range(n_shift):
            acc = acc + cw_ref[k:k + 1, lo:hi] * delayed[k * q:(k + 1) * q]
        return _silu(acc)

    xs = conv(xs_ref, xsp_ref, 0, SSD_D)
    bc = conv(bc_ref, bcp_ref, SSD_D, SSD_D + 2 * SSD_BCW)

    rows = lax.broadcasted_iota(I32, (q, q), 0)
    cols = lax.broadcasted_iota(I32, (q, q), 1)
    causal = rows >= cols
    tril = causal.astype(BF16)
    triu = (rows <= cols).astype(BF16)

    dt = _softplus(dt_ref[:, 0:SSD_HEADS] + dbr_ref[...])
    dA = dt * (-jnp.exp(alr_ref[...]))
    cum = sum(_dot(tril, p) for p in _split3(dA))
    dtT = _softplus(dtT_ref[...] + dbc_ref[...])
    dAT = dtT * (-jnp.exp(alc_ref[...]))
    cumT = sum(_dot(p, triu) for p in _split3(dAT))

    ecum = jnp.exp(cum)
    decs = jnp.exp(cum[q - 1:q, :] - cum)
    stack = jnp.concatenate([dt, ecum, decs], axis=0)
    ex = sum(_dot(p, e_ref[...]) for p in _split3(stack))
    dt_x, ecum_x, decs_x = ex[0:q], ex[q:2 * q], ex[2 * q:3 * q]
    cdec_x = ecum_x[q - 1:q, :]

    xdt = xs * dt_x
    xw = (xdt * decs_x).astype(BF16)
    xdt_b = xdt.astype(BF16)

    y_parts = []
    for g in range(SSD_GROUPS):
        b_g = bc[:, g * SSD_STATE:(g + 1) * SSD_STATE].astype(BF16)
        c_g = bc[:, SSD_BCW + g * SSD_STATE:SSD_BCW + (g + 1) * SSD_STATE].astype(BF16)
        gs = slice(g * SSD_GW, (g + 1) * SSD_GW)
        cb = _dot_nt(c_g, b_g)
        s_g = s_ref[g]
        y_g = _dot(c_g, s_g.astype(BF16)) * ecum_x[:, gs]
        diag = []
        for r in range(SSD_HEADS // SSD_GROUPS):
            hh = g * (SSD_HEADS // SSD_GROUPS) + r
            seg = cum[:, hh:hh + 1] - cumT[hh:hh + 1, :]
            dec = jnp.where(causal, jnp.exp(seg), 0.0)
            m = (cb * dec).astype(BF16)
            hs = slice(g * SSD_GW + r * SSD_HEAD_DIM, g * SSD_GW + (r + 1) * SSD_HEAD_DIM)
            diag.append(_dot(m, xdt_b[:, hs]))
        y_parts.append(y_g + jnp.concatenate(diag, axis=1))
        s_ref[g] = s_g * cdec_x[:, gs] + _dot_tn(b_g, xw[:, gs])
    y = jnp.concatenate(y_parts, axis=1) + xs * dexp_ref[...]
    o_ref[...] = _rms(y * _silu(z_ref[...].astype(F32)), nw_ref[...]).astype(o_ref.dtype)


def _ssd(proj0, dt_pad, dtT, conv_w, conv_b, a_log, dt_bias, d_skip, norm_w, bsz, seq):
    t = proj0.shape[0]
    nc = seq // CHUNK
    h = SSD_HEADS
    expand = jnp.repeat(jnp.eye(h, dtype=BF16), SSD_HEAD_DIM, axis=1)
    dexp = jnp.repeat(d_skip.astype(F32), SSD_HEAD_DIM).reshape(1, SSD_D)
    full = lambda shape: pl.BlockSpec(shape, lambda b, c: (0,) * len(shape))
    col = lambda j: pl.BlockSpec((CHUNK, SSD_D), lambda b, c: (b * nc + c, j))
    prev = lambda j: pl.BlockSpec((CHUNK, SSD_D), lambda b, c: (b * nc + jnp.maximum(c - 1, 0), j))
    return pl.pallas_call(
        _ssd_kernel,
        out_shape=jax.ShapeDtypeStruct((t, SSD_D), BF16),
        grid=(bsz, nc),
        in_specs=[
            col(0), col(1), col(2), prev(1), prev(2),
            pl.BlockSpec((CHUNK, LANES), lambda b, c: (b * nc + c, 0)),
            pl.BlockSpec((None, h, CHUNK), lambda b, c: (b, 0, c)),
            full((SSD_CONV, SSD_D + 2 * SSD_BCW)), full((1, SSD_D + 2 * SSD_BCW)),
            full((1, h)), full((h, 1)), full((1, h)), full((h, 1)),
            full((1, SSD_D)), full((1, SSD_D)), full((h, SSD_D)),
        ],
        out_specs=pl.BlockSpec((CHUNK, SSD_D), lambda b, c: (b * nc + c, 0)),
        scratch_shapes=[pltpu.VMEM((SSD_GROUPS, SSD_STATE, SSD_GW), F32)],
        compiler_params=_cparams(("parallel", "arbitrary")),
        name="ssd",
    )(proj0, proj0, proj0, proj0, proj0, dt_pad, dtT, conv_w, conv_b.reshape(1, -1),
      a_log.reshape(1, h), a_log.reshape(h, 1), dt_bias.reshape(1, h), dt_bias.reshape(h, 1),
      dexp, norm_w.reshape(1, SSD_D), expand)


RET_D = RET_HEADS * RET_DIM


def _ret_kernel(q_ref, k_ref, v_ref, g_ref, pos_ref, inv_ref, nw_ref, o_ref, r_ref):
    c = pl.program_id(1)
    q = CHUNK
    half = RET_DIM // 2

    @pl.when(c == 0)
    def _():
        r_ref[...] = jnp.zeros_like(r_ref)

    ang = pos_ref[...].astype(F32) * inv_ref[...]
    cs, sn = jnp.cos(ang), jnp.sin(ang)
    li = lax.broadcasted_iota(I32, (q, q), 0)
    si = lax.broadcasted_iota(I32, (q, q), 1)
    dist = (li - si).astype(F32)
    lcol = lax.broadcasted_iota(I32, (q, 1), 0).astype(F32)

    def rope(x):
        x1, x2 = x[:, :half], x[:, half:]
        return jnp.concatenate([x1 * cs - x2 * sn, x1 * sn + x2 * cs], axis=1)

    for h in range(RET_HEADS):
        lg = math.log(1.0 - 2.0 ** (-5.0 - h))
        hs = slice(h * RET_DIM, (h + 1) * RET_DIM)
        qr = rope(q_ref[:, hs].astype(F32)).astype(BF16)
        kr = rope(k_ref[:, hs].astype(F32)) * (RET_DIM ** -0.5)
        v = v_ref[:, hs]
        intra = jnp.where(dist >= 0.0, jnp.exp(dist * lg), 0.0)
        sc = _dot_nt(qr, kr.astype(BF16)) * intra
        r_h = r_ref[h]
        y = _dot(sc.astype(BF16), v) + _dot(qr, r_h.astype(BF16)) * jnp.exp((lcol + 1.0) * lg)
        kd = (kr * jnp.exp((q - 1.0 - lcol) * lg)).astype(BF16)
        r_ref[h] = r_h * math.exp(q * lg) + _dot_tn(kd, v)
        gate = _silu(g_ref[:, hs].astype(F32))
        o_ref[:, hs] = (gate * _rms(y, nw_ref[:, hs])).astype(o_ref.dtype)


def _retention(proj0, pos_col, norm_w, bsz, seq):
    t = proj0.shape[0]
    nc = seq // CHUNK
    half = RET_DIM // 2
    inv = (RET_THETA ** (-jnp.arange(half, dtype=F32) / half)).reshape(1, half)
    col = lambda j: pl.BlockSpec((CHUNK, RET_D), lambda b, c: (b * nc + c, j))
    return pl.pallas_call(
        _ret_kernel,
        out_shape=jax.ShapeDtypeStruct((t, RET_D), BF16),
        grid=(bsz, nc),
        in_specs=[
            col(0), col(1), col(2), col(3),
            pl.BlockSpec((CHUNK, 1), lambda b, c: (b * nc + c, 0)),
            pl.BlockSpec((1, half), lambda b, c: (0, 0)),
            pl.BlockSpec((1, RET_D), lambda b, c: (0, 0)),
        ],
        out_specs=pl.BlockSpec((CHUNK, RET_D), lambda b, c: (b * nc + c, 0)),
        scratch_shapes=[pltpu.VMEM((RET_HEADS, RET_DIM, RET_DIM), F32)],
        compiler_params=_cparams(("parallel", "arbitrary")),
        name="retention",
    )(proj0, proj0, proj0, proj0, pos_col, inv, norm_w.reshape(1, RET_D))


ATT_D = ATT_HEADS * ATT_DIM
IDX_D = IDX_HEADS * IDX_DIM
ATT_ROT = ATT_DIM // ROPE_FRACTION
IDX_ROT = IDX_DIM // ROPE_FRACTION
PREP_TM = 256
VT_ROWS = ATT_DIM + 16
Q_SCALE = ATT_DIM ** -0.5 * math.log2(math.e)


def _rope_factors(ang, half, period):
    lane = lax.broadcasted_iota(I32, ang.shape, 1) % period
    first = lane < half
    second = jnp.logical_and(lane >= half, lane < 2 * half)
    cs, sn = jnp.cos(ang), jnp.sin(ang)
    cfac = jnp.where(jnp.logical_or(first, second), cs, 1.0)
    sfac = jnp.where(first, -sn, jnp.where(second, sn, 0.0))
    return first, cfac, sfac, half


def _partial_rope(x, factors):
    first, cfac, sfac, half = factors
    partner = jnp.where(first, pltpu.roll(x, LANES - half, axis=1), pltpu.roll(x, half, axis=1))
    return x * cfac + partner * sfac


def _dsa_prep_kernel(p_ref, tail_ref, pos_ref, inva_ref, invi_ref, knw_ref,
                     qh_ref, qih_ref, k_ref, vT_ref, ki_ref, wi_ref):
    pos = pos_ref[...].astype(F32)
    rope_a = _rope_factors(pos * inva_ref[...], ATT_ROT // 2, ATT_DIM)
    rope_i = _rope_factors(pos * invi_ref[...], IDX_ROT // 2, IDX_DIM)
    first_a, cfac_a, sfac_a, half_a = rope_a
    rope_q = (first_a, cfac_a * Q_SCALE, sfac_a * Q_SCALE, half_a)
    o_k, o_v, o_qi = ATT_D, ATT_D + ATT_DIM, ATT_D + 2 * ATT_DIM

    for h in range(ATT_HEADS):
        x = p_ref[:, h * ATT_DIM:(h + 1) * ATT_DIM].astype(F32)
        qh_ref[h] = _partial_rope(x, rope_q).astype(BF16)
    for p in range(IDX_HEADS // 2):
        x = p_ref[:, o_qi + p * LANES:o_qi + (p + 1) * LANES].astype(F32)
        y = _partial_rope(x, rope_i).astype(BF16)
        qih_ref[2 * p] = y[:, :IDX_DIM]
        qih_ref[2 * p + 1] = y[:, IDX_DIM:]

    k = p_ref[:, o_k:o_v].astype(F32)
    k_ref[...] = _partial_rope(k, rope_a).astype(BF16)
    vT_ref[0:ATT_DIM, :] = jnp.transpose(p_ref[:, o_v:o_qi].astype(F32)).astype(BF16)
    ones_row = lax.broadcasted_iota(I32, (VT_ROWS - ATT_DIM, vT_ref.shape[1]), 0) == 0
    vT_ref[ATT_DIM:VT_ROWS, :] = jnp.where(ones_row, 1.0, 0.0).astype(BF16)

    tail = tail_ref[...]
    lane = lax.broadcasted_iota(I32, tail.shape, 1)
    ki = jnp.where(lane < IDX_DIM, tail, 0.0)
    ms = jnp.sum(ki * ki, axis=-1, keepdims=True) * (1.0 / IDX_DIM)
    kin = ki * lax.rsqrt(ms + EPS) * knw_ref[...]
    ki_ref[...] = _partial_rope(kin, rope_i)[:, :IDX_DIM].astype(BF16)
    wi_ref[...] = tail[:, IDX_DIM:IDX_DIM + IDX_HEADS] * (IDX_HEADS ** -0.5 * IDX_DIM ** -0.5)


def _dsa_prep(proj1, tail, pos_col, knw, bsz, seq):
    t = proj1.shape[0]
    tm = PREP_TM
    per_b = seq // tm
    ha, hi = ATT_ROT // 2, IDX_ROT // 2
    inv_a = ROPE_THETA ** (-jnp.arange(ha, dtype=F32) / ha)
    inv_i = ROPE_THETA ** (-jnp.arange(hi, dtype=F32) / hi)
    lane = jnp.arange(LANES)
    inva = jnp.where(lane < ATT_ROT, inv_a[lane % ha], 0.0).reshape(1, LANES)
    invi = jnp.where(lane % IDX_DIM < IDX_ROT, inv_i[lane % hi], 0.0).reshape(1, LANES)
    knw_pad = jnp.zeros((1, LANES), F32).at[0, :IDX_DIM].set(knw)
    return pl.pallas_call(
        _dsa_prep_kernel,
        out_shape=(
            jax.ShapeDtypeStruct((ATT_HEADS, t, ATT_DIM), BF16),
            jax.ShapeDtypeStruct((IDX_HEADS, t, IDX_DIM), BF16),
            jax.ShapeDtypeStruct((t, ATT_DIM), BF16),
            jax.ShapeDtypeStruct((bsz, VT_ROWS, seq), BF16),
            jax.ShapeDtypeStruct((t, IDX_DIM), BF16),
            jax.ShapeDtypeStruct((t, IDX_HEADS), F32),
        ),
        grid=(t // tm,),
        in_specs=[
            pl.BlockSpec((tm, proj1.shape[1]), lambda i: (i, 0)),
            pl.BlockSpec((tm, LANES), lambda i: (i, 0)),
            pl.BlockSpec((tm, 1), lambda i: (i, 0)),
            pl.BlockSpec((1, LANES), lambda i: (0, 0)),
            pl.BlockSpec((1, LANES), lambda i: (0, 0)),
            pl.BlockSpec((1, LANES), lambda i: (0, 0)),
        ],
        out_specs=(
            pl.BlockSpec((ATT_HEADS, tm, ATT_DIM), lambda i: (0, i, 0)),
            pl.BlockSpec((IDX_HEADS, tm, IDX_DIM), lambda i: (0, i, 0)),
            pl.BlockSpec((tm, ATT_DIM), lambda i: (i, 0)),
            pl.BlockSpec((None, VT_ROWS, tm), lambda i: (i // per_b, 0, i % per_b)),
            pl.BlockSpec((tm, IDX_DIM), lambda i: (i, 0)),
            pl.BlockSpec((tm, IDX_HEADS), lambda i: (i, 0)),
        ),
        compiler_params=_cparams(("parallel",)),
        name="dsa_prep",
    )(proj1, tail, pos_col, inva, invi, knw_pad)


DSA_KGROUP = 512
HEAD_PAIR = 2


REDUCE_SLABS = 8


def _col_reduce(x, reduce_fn, combine_fn):
    n = x.shape[0] // REDUCE_SLABS
    parts = [reduce_fn(x[g * n:(g + 1) * n], axis=0, keepdims=True) for g in range(REDUCE_SLABS)]
    while len(parts) > 1:
        parts = [combine_fn(parts[i], parts[i + 1]) for i in range(0, len(parts), 2)]
    return parts[0]


def _col_sum(x):
    return _col_reduce(x, jnp.sum, jnp.add)


def _col_max(x):
    return _col_reduce(x, jnp.max, jnp.maximum)


def _key_to_float(key):
    return pltpu.bitcast(jnp.where(key < 0, key ^ INT_MAX, key), F32)


def _dsa_body(lk, topk, j, qh_ref, qih_ref, wiT_ref, k_ref, vT_ref, ki_ref, o_ref):
    qb = Q_BLOCK
    kidx = lax.broadcasted_iota(I32, (lk, qb), 0)
    qpos = j * qb + lax.broadcasted_iota(I32, (lk, qb), 1)
    visible = kidx <= qpos

    ki = ki_ref[0:lk, :]
    score = jnp.zeros((lk, qb), F32)
    for h in range(0, IDX_HEADS, HEAD_PAIR):
        qi = qih_ref[h:h + HEAD_PAIR].reshape(HEAD_PAIR * qb, IDX_DIM)
        s = jnp.maximum(_dot_nt(ki, qi), 0.0)
        for r in range(HEAD_PAIR):
            score = score + s[:, r * qb:(r + 1) * qb] * wiT_ref[h + r:h + r + 1, :]
    score = jnp.where(visible, score, -jnp.inf)

    def count(mask):
        return _col_sum(jnp.where(mask, 1, 0).astype(I32))

    def bit_step(i, ans):
        trial = jnp.where(i == 0, jnp.zeros_like(ans), ans | jnp.left_shift(1, 31 - i))
        return jnp.where(count(score >= _key_to_float(trial)) >= topk, trial, ans)

    thr = _key_to_float(lax.fori_loop(0, 32, bit_step, jnp.full((1, qb), INT_MIN, I32)))
    above = score > thr
    equal = score == thr
    need = topk - count(above)
    over = count(equal) > need
    few = (j * qb + lax.broadcasted_iota(I32, (1, qb), 1)) < topk

    def tie_cut():
        nbits = max(lk - 1, 1).bit_length()

        def idx_step(i, ans):
            trial = ans | jnp.left_shift(1, nbits - 1 - i)
            c = count(jnp.logical_and(equal, kidx < trial))
            return jnp.where(c < need, trial, ans)

        cut = lax.fori_loop(0, nbits, idx_step, jnp.zeros((1, qb), I32))
        return jnp.where(over, cut, INT_MAX)

    cut = lax.cond(jnp.max(over.astype(I32)) > 0, tie_cut, lambda: jnp.full((1, qb), INT_MAX, I32))
    picked = jnp.logical_or(above, jnp.logical_and(equal, kidx <= cut))
    sel = jnp.logical_and(visible, jnp.logical_or(few, picked))
    bias = jnp.where(sel, 0.0, NEG_BIG)

    k = k_ref[0:lk, :]
    vT = vT_ref[:, 0:lk]
    bias2 = jnp.concatenate([bias] * HEAD_PAIR, axis=1)
    for h in range(0, ATT_HEADS, HEAD_PAIR):
        qq = qh_ref[h:h + HEAD_PAIR].reshape(HEAD_PAIR * qb, ATT_DIM)
        s = _dot_nt(k, qq) + bias2
        m = _col_max(s)
        p = jnp.exp2(s - m).astype(BF16)
        ov = _dot(vT, p)
        oT = ov[0:ATT_DIM] / ov[ATT_DIM:ATT_DIM + 1]
        for r in range(HEAD_PAIR):
            o_ref[:, (h + r) * ATT_DIM:(h + r + 1) * ATT_DIM] = (
                jnp.transpose(oT[:, r * qb:(r + 1) * qb]).astype(o_ref.dtype))


def _dsa_kernel(qh_ref, qih_ref, wiT_ref, k_ref, vT_ref, ki_ref, o_ref, *, seq):
    j = pl.program_id(1)
    per_group = DSA_KGROUP // Q_BLOCK
    topk = min(TOPK, seq // 4)
    for g in range(seq // DSA_KGROUP):
        @pl.when(j // per_group == g)
        def _(g=g):
            _dsa_body((g + 1) * DSA_KGROUP, topk, j, qh_ref, qih_ref, wiT_ref, k_ref, vT_ref, ki_ref, o_ref)


def _dsa_attention(qh, qih, wiT, k_r, vT, ki_r, bsz, seq):
    t = k_r.shape[0]
    nb = seq // Q_BLOCK
    return pl.pallas_call(
        functools.partial(_dsa_kernel, seq=seq),
        out_shape=jax.ShapeDtypeStruct((t, ATT_D), BF16),
        grid=(bsz, nb),
        in_specs=[
            pl.BlockSpec((ATT_HEADS, Q_BLOCK, ATT_DIM), lambda b, j: (0, b * nb + j, 0)),
            pl.BlockSpec((IDX_HEADS, Q_BLOCK, IDX_DIM), lambda b, j: (0, b * nb + j, 0)),
            pl.BlockSpec((IDX_HEADS, Q_BLOCK), lambda b, j: (0, b * nb + j)),
            pl.BlockSpec((seq, ATT_DIM), lambda b, j: (b, 0)),
            pl.BlockSpec((None, VT_ROWS, seq), lambda b, j: (b, 0, 0)),
            pl.BlockSpec((seq, IDX_DIM), lambda b, j: (b, 0)),
        ],
        out_specs=pl.BlockSpec((Q_BLOCK, ATT_D), lambda b, j: (b * nb + j, 0)),
        compiler_params=_cparams(("parallel", "parallel")),
        name="dsa_attention",
    )(qh, qih, wiT, k_r, vT, ki_r)


def _pad_rows(w, n):
    return jnp.pad(w, ((0, n - w.shape[0]), (0, 0)))


def _hybrid_mixer(x2, mods_l, norm_w_l, pos_col, w_in, e, conv_w, conv_b, a_log, dt_bias, d_skip, ssd_norm_w,
                  ret_norm_w, w_out, bsz, seq):
    h = _norm_mod(x2, norm_w_l[2], mods_l, 1, seq)
    tm = min(MM_TM, seq)
    n_zx = SSD_D + SSD_D + 2 * SSD_BCW
    ret_lo = n_zx + SSD_HEADS
    w_t = jnp.swapaxes(w_in, 1, 2)
    w_dt = _pad_rows(w_t[e, n_zx:ret_lo], LANES)
    proj_zx, dt_pad = _matmul_wt(h, w_t, BF16, tm, 1024, n_zx, e, side_w=w_dt)
    proj_ret = _matmul_wt(h, w_t, BF16, tm, 1024, w_t.shape[1] - ret_lo, e, row0=ret_lo)
    dtT = jnp.swapaxes(dt_pad[:, :SSD_HEADS].reshape(bsz, seq, SSD_HEADS), 1, 2)
    ya = _ssd(proj_zx, dt_pad, dtT, conv_w, conv_b, a_log, dt_bias, d_skip, ssd_norm_w, bsz, seq)
    yb = _retention(proj_ret, pos_col, ret_norm_w, bsz, seq)
    return _proj_resid([ya, yb], w_out[e].astype(BF16), x2, mods_l, 1, norm_w_l[3], seq)


def _dsa_mixer(x2, mods_l, norm_w_l, pos_col, w_in, o, knw, w_out, bsz, seq):
    h = _norm_mod(x2, norm_w_l[2], mods_l, 1, seq)
    tm = min(MM_TM, seq)
    n_main = ATT_D + 2 * ATT_DIM + IDX_D
    w_t = jnp.swapaxes(w_in, 1, 2)
    w_tail = _pad_rows(w_t[o, n_main:], LANES)
    proj1, tail = _matmul_wt(h, w_t, BF16, tm, n_main // 2, n_main, o, side_w=w_tail)
    qh, qih, k_r, vT, ki_r, wi = _dsa_prep(proj1, tail, pos_col, knw, bsz, seq)
    att = _dsa_attention(qh, qih, wi.T, k_r, vT, ki_r, bsz, seq)
    return _proj_resid([att], w_out[o].astype(BF16), x2, mods_l, 1, norm_w_l[3], seq)


def kernel(x, c, positions, mod_w, mod_b, norm_w, ffn_w_gate, ffn_w_up, ffn_w_down, hy_w_in, hy_conv_w,
           hy_conv_b, ssd_A_log, ssd_dt_bias, ssd_D, ssd_norm_w, ret_norm_w, hy_w_out, dsa_w_in,
           idx_k_norm_w, dsa_w_out):
    bsz, seq, d = x.shape
    depth = mod_w.shape[0]
    x2 = x.reshape(bsz * seq, d)
    pos_col = positions.reshape(bsz * seq, 1)
    mods = _mods(c, mod_w, mod_b)
    for i in range(depth):
        ml, nw = mods[i], norm_w[i]
        x2 = _ffn(x2, ml, 0, nw[0], nw[1], ffn_w_gate, ffn_w_up, ffn_w_down, i, 0, seq)
        if i % 2 == 0:
            e = i // 2
            x2 = _hybrid_mixer(x2, ml, nw, pos_col, hy_w_in, e, hy_conv_w[e], hy_conv_b[e], ssd_A_log[e],
                               ssd_dt_bias[e], ssd_D[e], ssd_norm_w[e], ret_norm_w[e], hy_w_out, bsz, seq)
        else:
            o = i // 2
            x2 = _dsa_mixer(x2, ml, nw, pos_col, dsa_w_in, o, idx_k_norm_w[o], dsa_w_out, bsz, seq)
        x2 = _ffn(x2, ml, 2, nw[4], nw[5], ffn_w_gate, ffn_w_up, ffn_w_down, i, 1, seq)
    return x2.reshape(bsz, seq, d)
```

```python
import functools
import math

import jax
import jax.numpy as jnp
from jax import lax
from jax.experimental import pallas as pl
from jax.experimental.pallas import tpu as pltpu

F32 = jnp.float32
BF16 = jnp.bfloat16
I32 = jnp.int32

EPS = 1e-6
SSD_HEADS = 32
SSD_HEAD_DIM = 64
SSD_GROUPS = 8
SSD_STATE = 128
SSD_CONV = 4
CHUNK = 128
RET_HEADS = 8
RET_DIM = 256
RET_THETA = 10000.0
ATT_HEADS = 16
ATT_DIM = 128
IDX_HEADS = 16
IDX_DIM = 64
TOPK = 256
Q_BLOCK = 128
ROPE_THETA = 500000.0
ROPE_FRACTION = 4

V7X_VMEM_BYTES = 64 * 1024 * 1024
VMEM_LIMIT = V7X_VMEM_BYTES - 8 * 1024 * 1024
LANES = 128
SUBLANES = 8

INT_MIN = -(2 ** 31)
INT_MAX = 2 ** 31 - 1
NEG_BIG = -1e30


def _cparams(sem):
    return pltpu.CompilerParams(dimension_semantics=sem, vmem_limit_bytes=VMEM_LIMIT)


def _sigmoid(x):
    return 1.0 / (1.0 + jnp.exp(-x))


def _silu(x):
    return x * _sigmoid(x)


def _softplus(x):
    return jnp.maximum(x, 0.0) + jnp.log1p(jnp.exp(-jnp.abs(x)))


def _rms(x, w):
    return x * lax.rsqrt(jnp.mean(x * x, axis=-1, keepdims=True) + EPS) * w


def _dot(a, b):
    return jnp.dot(a, b, preferred_element_type=F32)


def _dot_nt(a, b):
    return lax.dot_general(a, b, (((1,), (1,)), ((), ())), preferred_element_type=F32)


def _dot_tn(a, b):
    return lax.dot_general(a, b, (((0,), (0,)), ((), ())), preferred_element_type=F32)


def _split3(x):
    hi = x.astype(BF16)
    r = x - hi.astype(F32)
    mid = r.astype(BF16)
    lo = (r - mid.astype(F32)).astype(BF16)
    return hi, mid, lo


def _row_chunks(n_rows, rc, fn):
    def body(i, carry):
        fn(pl.ds(pl.multiple_of(i * rc, rc), rc))
        return carry
    lax.fori_loop(0, n_rows // rc, body, 0)


ROW_CHUNK = 128


def _inv_rms_rows(src_ref, r_ref, n_rows):
    def body(rows):
        x = src_ref[rows, :]
        r_ref[rows, :] = lax.rsqrt(jnp.mean(x * x, axis=-1, keepdims=True) + EPS)
    _row_chunks(n_rows, ROW_CHUNK, body)


def _mods_kernel(c_ref, w_ref, b_ref, o_ref):
    cond = _silu(c_ref[...]).astype(BF16)
    o_ref[...] = _dot(cond, w_ref[...].astype(BF16)) + b_ref[...]


def _mods(c, mod_w, mod_b):
    depth, d, n = mod_w.shape
    bsz = c.shape[0]
    c8 = jnp.zeros((SUBLANES, d), F32).at[:bsz].set(c)
    tn = 1024
    out = pl.pallas_call(
        _mods_kernel,
        out_shape=jax.ShapeDtypeStruct((depth, SUBLANES, n), F32),
        grid=(depth, n // tn),
        in_specs=[
            pl.BlockSpec((SUBLANES, d), lambda i, j: (0, 0)),
            pl.BlockSpec((None, d, tn), lambda i, j: (i, 0, j)),
            pl.BlockSpec((None, 1, tn), lambda i, j: (i, 0, j)),
        ],
        out_specs=pl.BlockSpec((None, SUBLANES, tn), lambda i, j: (i, 0, j)),
        compiler_params=_cparams(("parallel", "parallel")),
        name="mods",
    )(c8, mod_w, mod_b.reshape(depth, 1, n))
    return out[:, :bsz].reshape(depth, bsz * 3, 3, d)


NORM_TM = 512


def _norm_mod_kernel(x_ref, w_ref, m_ref, o_ref, r_ref):
    n_rows = x_ref.shape[0]
    _inv_rms_rows(x_ref, r_ref, n_rows)
    wmod = w_ref[...] * (1.0 + m_ref[1:2, :])
    shift = m_ref[0:1, :]

    def body(rows):
        o_ref[rows, :] = (x_ref[rows, :] * r_ref[rows, :] * wmod + shift).astype(o_ref.dtype)
    _row_chunks(n_rows, ROW_CHUNK, body)


def _norm_mod(x2, pre_w, mods_l, sub, seq):
    t, d = x2.shape
    tm = min(NORM_TM, seq)
    per_b = seq // tm
    return pl.pallas_call(
        _norm_mod_kernel,
        out_shape=jax.ShapeDtypeStruct((t, d), BF16),
        grid=(t // tm,),
        in_specs=[
            pl.BlockSpec((tm, d), lambda i: (i, 0)),
            pl.BlockSpec((1, d), lambda i: (0, 0)),
            pl.BlockSpec((None, 3, d), lambda i: ((i // per_b) * 3 + sub, 0, 0)),
        ],
        out_specs=pl.BlockSpec((tm, d), lambda i: (i, 0)),
        scratch_shapes=[pltpu.VMEM((tm, 1), F32)],
        compiler_params=_cparams(("parallel",)),
        name="norm_mod",
    )(x2, pre_w.reshape(1, d), mods_l)


MM_TM = 1024


def _mm_kernel(a_ref, w_ref, o_ref):
    o_ref[...] = _dot_nt(a_ref[...], w_ref[...].astype(BF16)).astype(o_ref.dtype)


def _mm_side_kernel(a_ref, w_ref, ws_ref, o_ref, os_ref):
    _mm_kernel(a_ref, w_ref, o_ref)

    @pl.when(pl.program_id(1) == 0)
    def _():
        os_ref[...] = _dot_nt(a_ref[...], ws_ref[...].astype(BF16))


def _matmul_wt(a, w_t, out_dtype, tm, tn, n_out, lead, row0=0, side_w=None):
    m, k = a.shape
    if row0 % tn == 0:
        w_spec = pl.BlockSpec((None, tn, k), lambda i, j: (lead, row0 // tn + j, 0))
    else:
        w_t = w_t[lead]
        assert row0 % SUBLANES == 0 and tn % SUBLANES == 0
        w_spec = pl.BlockSpec((pl.Element(tn), pl.Element(k)),
                              lambda i, j: ((row0 // SUBLANES + j * (tn // SUBLANES)) * SUBLANES, 0))
    a_spec = pl.BlockSpec((tm, k), lambda i, j: (i, 0))
    o_spec = pl.BlockSpec((tm, tn), lambda i, j: (i, j))
    if side_w is None:
        return pl.pallas_call(
            _mm_kernel,
            out_shape=jax.ShapeDtypeStruct((m, n_out), out_dtype),
            grid=(m // tm, n_out // tn),
            in_specs=[a_spec, w_spec],
            out_specs=o_spec,
            compiler_params=_cparams(("parallel", "parallel")),
            name="matmul",
        )(a, w_t)
    return pl.pallas_call(
        _mm_side_kernel,
        out_shape=(jax.ShapeDtypeStruct((m, n_out), out_dtype), jax.ShapeDtypeStruct((m, LANES), F32)),
        grid=(m // tm, n_out // tn),
        in_specs=[a_spec, w_spec, pl.BlockSpec((LANES, k), lambda i, j: (0, 0))],
        out_specs=(o_spec, pl.BlockSpec((tm, LANES), lambda i, j: (i, 0))),
        compiler_params=_cparams(("parallel", "arbitrary")),
        name="matmul_side",
    )(a, w_t, side_w)


FFN_TM = 1024
FFN_TF = 512
FFN_SUB = 256


def _ffn_kernel(x_ref, m_ref, prew_ref, postw_ref, wg_ref, wu_ref, wd_ref, o_ref, h_ref, r_ref, *, nf, tm, tf):
    f = pl.program_id(1)

    @pl.when(f == 0)
    def _():
        _inv_rms_rows(x_ref, r_ref, tm)
        wmod = prew_ref[...] * (1.0 + m_ref[1:2, :])
        shift = m_ref[0:1, :]

        def pro(rows):
            h_ref[rows, :] = (x_ref[rows, :] * r_ref[rows, :] * wmod + shift).astype(BF16)
            o_ref[rows, :] = jnp.zeros((ROW_CHUNK, o_ref.shape[1]), F32)
        _row_chunks(tm, ROW_CHUNK, pro)

    h = h_ref[...]
    acts = []
    for c in range(tf // FFN_SUB):
        cs = slice(c * FFN_SUB, (c + 1) * FFN_SUB)
        g = _dot(h, wg_ref[:, cs].astype(BF16))
        u = _dot(h, wu_ref[:, cs].astype(BF16))
        acts.append((_silu(g) * u).astype(BF16))
    o_ref[...] += _dot(jnp.concatenate(acts, axis=1), wd_ref[...].astype(BF16))

    @pl.when(f == nf - 1)
    def _():
        _inv_rms_rows(o_ref, r_ref, tm)
        gmod = postw_ref[...] * (0.5 * m_ref[2:3, :])

        def epi(rows):
            o_ref[rows, :] = x_ref[rows, :] + o_ref[rows, :] * r_ref[rows, :] * gmod
        _row_chunks(tm, ROW_CHUNK, epi)


def _ffn(x2, mods_l, sub, pre_w, post_w, w_gate, w_up, w_down, layer, which, seq):
    t, d = x2.shape
    dff = w_gate.shape[-1]
    tm, tf = min(FFN_TM, seq), FFN_TF
    nf = dff // tf
    per_b = seq // tm
    one = pl.Buffered(1)
    return pl.pallas_call(
        functools.partial(_ffn_kernel, nf=nf, tm=tm, tf=tf),
        out_shape=jax.ShapeDtypeStruct((t, d), F32),
        grid=(t // tm, nf),
        in_specs=[
            pl.BlockSpec((tm, d), lambda i, f: (i, 0), pipeline_mode=one),
            pl.BlockSpec((None, 3, d), lambda i, f: ((i // per_b) * 3 + sub, 0, 0)),
            pl.BlockSpec((1, d), lambda i, f: (0, 0)),
            pl.BlockSpec((1, d), lambda i, f: (0, 0)),
            pl.BlockSpec((None, None, d, tf), lambda i, f: (layer, which, 0, f)),
            pl.BlockSpec((None, None, d, tf), lambda i, f: (layer, which, 0, f)),
            pl.BlockSpec((None, None, tf, d), lambda i, f: (layer, which, f, 0)),
        ],
        out_specs=pl.BlockSpec((tm, d), lambda i, f: (i, 0), pipeline_mode=one),
        scratch_shapes=[pltpu.VMEM((tm, d), BF16), pltpu.VMEM((tm, 1), F32)],
        compiler_params=_cparams(("parallel", "arbitrary")),
        name="ffn",
    )(x2, mods_l, pre_w.reshape(1, d), post_w.reshape(1, d), w_gate, w_up, w_down)


PROJ_TM = 512


def _proj_resid_kernel(*refs, n_src, tm):
    a_refs = refs[:n_src]
    w_ref, x_ref, m_ref, postw_ref, o_ref, r_ref = refs[n_src:]
    ks = a_refs[0].shape[1]
    acc = _dot(a_refs[0][...], w_ref[0:ks, :])
    for s in range(1, n_src):
        acc = acc + _dot(a_refs[s][...], w_ref[s * ks:(s + 1) * ks, :])
    o_ref[...] = acc

    _inv_rms_rows(o_ref, r_ref, tm)
    gmod = postw_ref[...] * m_ref[2:3, :]

    def epi(rows):
        o_ref[rows, :] = x_ref[rows, :] + o_ref[rows, :] * r_ref[rows, :] * gmod
    _row_chunks(tm, ROW_CHUNK, epi)


def _proj_resid(srcs, w_bf, x2, mods_l, sub, post_w, seq):
    t, d = x2.shape
    n_src = len(srcs)
    ks = srcs[0].shape[1]
    tm = min(PROJ_TM, seq)
    per_b = seq // tm
    return pl.pallas_call(
        functools.partial(_proj_resid_kernel, n_src=n_src, tm=tm),
        out_shape=jax.ShapeDtypeStruct((t, d), F32),
        grid=(t // tm,),
        in_specs=[pl.BlockSpec((tm, ks), lambda i: (i, 0)) for _ in range(n_src)] + [
            pl.BlockSpec((n_src * ks, d), lambda i: (0, 0), pipeline_mode=pl.Buffered(1)),
            pl.BlockSpec((tm, d), lambda i: (i, 0)),
            pl.BlockSpec((None, 3, d), lambda i: ((i // per_b) * 3 + sub, 0, 0)),
            pl.BlockSpec((1, d), lambda i: (0, 0)),
        ],
        out_specs=pl.BlockSpec((tm, d), lambda i: (i, 0)),
        scratch_shapes=[pltpu.VMEM((tm, 1), F32)],
        compiler_params=_cparams(("parallel",)),
        name="proj_resid",
    )(*srcs, w_bf, x2, mods_l, post_w.reshape(1, d))


SSD_D = SSD_HEADS * SSD_HEAD_DIM
SSD_GW = SSD_D // SSD_GROUPS
SSD_BCW = SSD_GROUPS * SSD_STATE


def _ssd_kernel(z_ref, xs_ref, bc_ref, xsp_ref, bcp_ref, dt_ref, dtT_ref, cw_ref, cb_ref, alr_ref, alc_ref,
                dbr_ref, dbc_ref, dexp_ref, nw_ref, e_ref, o_ref, s_ref):
    c = pl.program_id(1)
    q = CHUNK

    @pl.when(c == 0)
    def _():
        s_ref[...] = jnp.zeros_like(s_ref)

    n_shift = SSD_CONV - 1
    tt = lax.broadcasted_iota(I32, (n_shift * q, 2 * q), 0)
    rr = lax.broadcasted_iota(I32, (n_shift * q, 2 * q), 1)
    shift_mat = (rr == (tt % q) + q - n_shift + tt // q).astype(BF16)

    def conv(cur_ref, prev_ref, lo, hi):
        cur = cur_ref[...]
        prev = jnp.where(c > 0, prev_ref[...], jnp.zeros_like(cur))
        delayed = _dot(shift_mat, jnp.concatenate([prev, cur], axis=0))
        acc = cb_ref[:, lo:hi] + cw_ref[n_shift:n_shift + 1, lo:hi] * cur.astype(F32)
        for k in range(n_shift):
            acc = acc + cw_ref[k:k + 1, lo:hi] * delayed[k * q:(k + 1) * q]
        return _silu(acc)

    xs = conv(xs_ref, xsp_ref, 0, SSD_D)
    bc = conv(bc_ref, bcp_ref, SSD_D, SSD_D + 2 * SSD_BCW)

    rows = lax.broadcasted_iota(I32, (q, q), 0)
    cols = lax.broadcasted_iota(I32, (q, q), 1)
    causal = rows >= cols
    tril = causal.astype(BF16)
    triu = (rows <= cols).astype(BF16)

    dt = _softplus(dt_ref[:, 0:SSD_HEADS] + dbr_ref[...])
    dA = dt * (-jnp.exp(alr_ref[...]))
    cum = sum(_dot(tril, p) for p in _split3(dA))
    dtT = _softplus(dtT_ref[...] + dbc_ref[...])
    dAT = dtT * (-jnp.exp(alc_ref[...]))
    cumT = sum(_dot(p, triu) for p in _split3(dAT))

    ecum = jnp.exp(cum)
    decs = jnp.exp(cum[q - 1:q, :] - cum)
    stack = jnp.concatenate([dt, ecum, decs], axis=0)
    ex = sum(_dot(p, e_ref[...]) for p in _split3(stack))
    dt_x, ecum_x, decs_x = ex[0:q], ex[q:2 * q], ex[2 * q:3 * q]
    cdec_x = ecum_x[q - 1:q, :]

    xdt = xs * dt_x
    xw = (xdt * decs_x).astype(BF16)
    xdt_b = xdt.astype(BF16)

    y_parts = []
    for g in range(SSD_GROUPS):
        b_g = bc[:, g * SSD_STATE:(g + 1) * SSD_STATE].astype(BF16)
        c_g = bc[:, SSD_BCW + g * SSD_STATE:SSD_BCW + (g + 1) * SSD_STATE].astype(BF16)
        gs = slice(g * SSD_GW, (g + 1) * SSD_GW)
        cb = _dot_nt(c_g, b_g)
        s_g = s_ref[g]
        y_g = _dot(c_g, s_g.astype(BF16)) * ecum_x[:, gs]
        diag = []
        for r in range(SSD_HEADS // SSD_GROUPS):
            hh = g * (SSD_HEADS // SSD_GROUPS) + r
            seg = cum[:, hh:hh + 1] - cumT[hh:hh + 1, :]
            dec = jnp.where(causal, jnp.exp(seg), 0.0)
            m = (cb * dec).astype(BF16)
            hs = slice(g * SSD_GW + r * SSD_HEAD_DIM, g * SSD_GW + (r + 1) * SSD_HEAD_DIM)
            diag.append(_dot(m, xdt_b[:, hs]))
        y_parts.append(y_g + jnp.concatenate(diag, axis=1))
        s_ref[g] = s_g * cdec_x[:, gs] + _dot_tn(b_g, xw[:, gs])
    y = jnp.concatenate(y_parts, axis=1) + xs * dexp_ref[...]
    o_ref[...] = _rms(y * _silu(z_ref[...].astype(F32)), nw_ref[...]).astype(o_ref.dtype)


def _ssd(proj0, dt_pad, dtT, conv_w, conv_b, a_log, dt_bias, d_skip, norm_w, bsz, seq):
    t = proj0.shape[0]
    nc = seq // CHUNK
    h = SSD_HEADS
    expand = jnp.repeat(jnp.eye(h, dtype=BF16), SSD_HEAD_DIM, axis=1)
    dexp = jnp.repeat(d_skip.astype(F32), SSD_HEAD_DIM).reshape(1, SSD_D)
    full = lambda shape: pl.BlockSpec(shape, lambda b, c: (0,) * len(shape))
    col = lambda j: pl.BlockSpec((CHUNK, SSD_D), lambda b, c: (b * nc + c, j))
    prev = lambda j: pl.BlockSpec((CHUNK, SSD_D), lambda b, c: (b * nc + jnp.maximum(c - 1, 0), j))
    return pl.pallas_call(
        _ssd_kernel,
        out_shape=jax.ShapeDtypeStruct((t, SSD_D), BF16),
        grid=(bsz, nc),
        in_specs=[
            col(0), col(1), col(2), prev(1), prev(2),
            pl.BlockSpec((CHUNK, LANES), lambda b, c: (b * nc + c, 0)),
            pl.BlockSpec((None, h, CHUNK), lambda b, c: (b, 0, c)),
            full((SSD_CONV, SSD_D + 2 * SSD_BCW)), full((1, SSD_D + 2 * SSD_BCW)),
            full((1, h)), full((h, 1)), full((1, h)), full((h, 1)),
            full((1, SSD_D)), full((1, SSD_D)), full((h, SSD_D)),
        ],
        out_specs=pl.BlockSpec((CHUNK, SSD_D), lambda b, c: (b * nc + c, 0)),
        scratch_shapes=[pltpu.VMEM((SSD_GROUPS, SSD_STATE, SSD_GW), F32)],
        compiler_params=_cparams(("parallel", "arbitrary")),
        name="ssd",
    )(proj0, proj0, proj0, proj0, proj0, dt_pad, dtT, conv_w, conv_b.reshape(1, -1),
      a_log.reshape(1, h), a_log.reshape(h, 1), dt_bias.reshape(1, h), dt_bias.reshape(h, 1),
      dexp, norm_w.reshape(1, SSD_D), expand)


RET_D = RET_HEADS * RET_DIM


def _ret_kernel(q_ref, k_ref, v_ref, g_ref, pos_ref, inv_ref, nw_ref, o_ref, r_ref):
    c = pl.program_id(1)
    q = CHUNK
    half = RET_DIM // 2

    @pl.when(c == 0)
    def _():
        r_ref[...] = jnp.zeros_like(r_ref)

    ang = pos_ref[...].astype(F32) * inv_ref[...]
    cs, sn = jnp.cos(ang), jnp.sin(ang)
    li = lax.broadcasted_iota(I32, (q, q), 0)
    si = lax.broadcasted_iota(I32, (q, q), 1)
    dist = (li - si).astype(F32)
    lcol = lax.broadcasted_iota(I32, (q, 1), 0).astype(F32)

    def rope(x):
        x1, x2 = x[:, :half], x[:, half:]
        return jnp.concatenate([x1 * cs - x2 * sn, x1 * sn + x2 * cs], axis=1)

    for h in range(RET_HEADS):
        lg = math.log(1.0 - 2.0 ** (-5.0 - h))
        hs = slice(h * RET_DIM, (h + 1) * RET_DIM)
        qr = rope(q_ref[:, hs].astype(F32)).astype(BF16)
        kr = rope(k_ref[:, hs].astype(F32)) * (RET_DIM ** -0.5)
        v = v_ref[:, hs]
        intra = jnp.where(dist >= 0.0, jnp.exp(dist * lg), 0.0)
        sc = _dot_nt(qr, kr.astype(BF16)) * intra
        r_h = r_ref[h]
        y = _dot(sc.astype(BF16), v) + _dot(qr, r_h.astype(BF16)) * jnp.exp((lcol + 1.0) * lg)
        kd = (kr * jnp.exp((q - 1.0 - lcol) * lg)).astype(BF16)
        r_ref[h] = r_h * math.exp(q * lg) + _dot_tn(kd, v)
        gate = _silu(g_ref[:, hs].astype(F32))
        o_ref[:, hs] = (gate * _rms(y, nw_ref[:, hs])).astype(o_ref.dtype)


def _retention(proj0, pos_col, norm_w, bsz, seq):
    t = proj0.shape[0]
    nc = seq // CHUNK
    half = RET_DIM // 2
    inv = (RET_THETA ** (-jnp.arange(half, dtype=F32) / half)).reshape(1, half)
    col = lambda j: pl.BlockSpec((CHUNK, RET_D), lambda b, c: (b * nc + c, j))
    return pl.pallas_call(
        _ret_kernel,
        out_shape=jax.ShapeDtypeStruct((t, RET_D), BF16),
        grid=(bsz, nc),
        in_specs=[
            col(0), col(1), col(2), col(3),
            pl.BlockSpec((CHUNK, 1), lambda b, c: (b * nc + c, 0)),
            pl.BlockSpec((1, half), lambda b, c: (0, 0)),
            pl.BlockSpec((1, RET_D), lambda b, c: (0, 0)),
        ],
        out_specs=pl.BlockSpec((CHUNK, RET_D), lambda b, c: (b * nc + c, 0)),
        scratch_shapes=[pltpu.VMEM((RET_HEADS, RET_DIM, RET_DIM), F32)],
        compiler_params=_cparams(("parallel", "arbitrary")),
        name="retention",
    )(proj0, proj0, proj0, proj0, pos_col, inv, norm_w.reshape(1, RET_D))


ATT_D = ATT_HEADS * ATT_DIM
IDX_D = IDX_HEADS * IDX_DIM
ATT_ROT = ATT_DIM // ROPE_FRACTION
IDX_ROT = IDX_DIM // ROPE_FRACTION
PREP_TM = 256
VT_ROWS = ATT_DIM + 16
Q_SCALE = ATT_DIM ** -0.5 * math.log2(math.e)


def _rope_factors(ang, half, period):
    lane = lax.broadcasted_iota(I32, ang.shape, 1) % period
    first = lane < half
    second = jnp.logical_and(lane >= half, lane < 2 * half)
    cs, sn = jnp.cos(ang), jnp.sin(ang)
    cfac = jnp.where(jnp.logical_or(first, second), cs, 1.0)
    sfac = jnp.where(first, -sn, jnp.where(second, sn, 0.0))
    return first, cfac, sfac, half


def _partial_rope(x, factors):
    first, cfac, sfac, half = factors
    partner = jnp.where(first, pltpu.roll(x, LANES - half, axis=1), pltpu.roll(x, half, axis=1))
    return x * cfac + partner * sfac


def _dsa_prep_kernel(p_ref, tail_ref, pos_ref, inva_ref, invi_ref, knw_ref,
                     qh_ref, qih_ref, k_ref, vT_ref, ki_ref, wi_ref):
    pos = pos_ref[...].astype(F32)
    rope_a = _rope_factors(pos * inva_ref[...], ATT_ROT // 2, ATT_DIM)
    rope_i = _rope_factors(pos * invi_ref[...], IDX_ROT // 2, IDX_DIM)
    first_a, cfac_a, sfac_a, half_a = rope_a
    rope_q = (first_a, cfac_a * Q_SCALE, sfac_a * Q_SCALE, half_a)
    o_k, o_v, o_qi = ATT_D, ATT_D + ATT_DIM, ATT_D + 2 * ATT_DIM

    for h in range(ATT_HEADS):
        x = p_ref[:, h * ATT_DIM:(h + 1) * ATT_DIM].astype(F32)
        qh_ref[h] = _partial_rope(x, rope_q).astype(BF16)
    for p in range(IDX_HEADS // 2):
        x = p_ref[:, o_qi + p * LANES:o_qi + (p + 1) * LANES].astype(F32)
        y = _partial_rope(x, rope_i).astype(BF16)
        qih_ref[2 * p] = y[:, :IDX_DIM]
        qih_ref[2 * p + 1] = y[:, IDX_DIM:]

    k = p_ref[:, o_k:o_v].astype(F32)
    k_ref[...] = _partial_rope(k, rope_a).astype(BF16)
    vT_ref[0:ATT_DIM, :] = jnp.transpose(p_ref[:, o_v:o_qi].astype(F32)).astype(BF16)
    ones_row = lax.broadcasted_iota(I32, (VT_ROWS - ATT_DIM, vT_ref.shape[1]), 0) == 0
    vT_ref[ATT_DIM:VT_ROWS, :] = jnp.where(ones_row, 1.0, 0.0).astype(BF16)

    tail = tail_ref[...]
    lane = lax.broadcasted_iota(I32, tail.shape, 1)
    ki = jnp.where(lane < IDX_DIM, tail, 0.0)
    ms = jnp.sum(ki * ki, axis=-1, keepdims=True) * (1.0 / IDX_DIM)
    kin = ki * lax.rsqrt(ms + EPS) * knw_ref[...]
    ki_ref[...] = _partial_rope(kin, rope_i)[:, :IDX_DIM].astype(BF16)
    wi_ref[...] = tail[:, IDX_DIM:IDX_DIM + IDX_HEADS] * (IDX_HEADS ** -0.5 * IDX_DIM ** -0.5)


def _dsa_prep(proj1, tail, pos_col, knw, bsz, seq):
    t = proj1.shape[0]
    tm = PREP_TM
    per_b = seq // tm
    ha, hi = ATT_ROT // 2, IDX_ROT // 2
    inv_a = ROPE_THETA ** (-jnp.arange(ha, dtype=F32) / ha)
    inv_i = ROPE_THETA ** (-jnp.arange(hi, dtype=F32) / hi)
    lane = jnp.arange(LANES)
    inva = jnp.where(lane < ATT_ROT, inv_a[lane % ha], 0.0).reshape(1, LANES)
    invi = jnp.where(lane % IDX_DIM < IDX_ROT, inv_i[lane % hi], 0.0).reshape(1, LANES)
    knw_pad = jnp.zeros((1, LANES), F32).at[0, :IDX_DIM].set(knw)
    return pl.pallas_call(
        _dsa_prep_kernel,
        out_shape=(
            jax.ShapeDtypeStruct((ATT_HEADS, t, ATT_DIM), BF16),
            jax.ShapeDtypeStruct((IDX_HEADS, t, IDX_DIM), BF16),
            jax.ShapeDtypeStruct((t, ATT_DIM), BF16),
            jax.ShapeDtypeStruct((bsz, VT_ROWS, seq), BF16),
            jax.ShapeDtypeStruct((t, IDX_DIM), BF16),
            jax.ShapeDtypeStruct((t, IDX_HEADS), F32),
        ),
        grid=(t // tm,),
        in_specs=[
            pl.BlockSpec((tm, proj1.shape[1]), lambda i: (i, 0)),
            pl.BlockSpec((tm, LANES), lambda i: (i, 0)),
            pl.BlockSpec((tm, 1), lambda i: (i, 0)),
            pl.BlockSpec((1, LANES), lambda i: (0, 0)),
            pl.BlockSpec((1, LANES), lambda i: (0, 0)),
            pl.BlockSpec((1, LANES), lambda i: (0, 0)),
        ],
        out_specs=(
            pl.BlockSpec((ATT_HEADS, tm, ATT_DIM), lambda i: (0, i, 0)),
            pl.BlockSpec((IDX_HEADS, tm, IDX_DIM), lambda i: (0, i, 0)),
            pl.BlockSpec((tm, ATT_DIM), lambda i: (i, 0)),
            pl.BlockSpec((None, VT_ROWS, tm), lambda i: (i // per_b, 0, i % per_b)),
            pl.BlockSpec((tm, IDX_DIM), lambda i: (i, 0)),
            pl.BlockSpec((tm, IDX_HEADS), lambda i: (i, 0)),
        ),
        compiler_params=_cparams(("parallel",)),
        name="dsa_prep",
    )(proj1, tail, pos_col, inva, invi, knw_pad)


DSA_KGROUP = 512
HEADS_PER_DOT = 8


REDUCE_SLABS = 8


def _col_reduce(x, reduce_fn, combine_fn):
    n = x.shape[0] // REDUCE_SLABS
    parts = [reduce_fn(x[g * n:(g + 1) * n], axis=0, keepdims=True) for g in range(REDUCE_SLABS)]
    while len(parts) > 1:
        parts = [combine_fn(parts[i], parts[i + 1]) for i in range(0, len(parts), 2)]
    return parts[0]


def _col_sum(x):
    return _col_reduce(x, jnp.sum, jnp.add)


def _col_max(x):
    return _col_reduce(x, jnp.max, jnp.maximum)


def _key_to_float(key):
    return pltpu.bitcast(jnp.where(key < 0, key ^ INT_MAX, key), F32)


def _dsa_body(lk, topk, j, qh_ref, qih_ref, wiT_ref, k_ref, vT_ref, ki_ref, o_ref):
    qb = Q_BLOCK
    kidx = lax.broadcasted_iota(I32, (lk, qb), 0)
    qpos = j * qb + lax.broadcasted_iota(I32, (lk, qb), 1)
    visible = kidx <= qpos

    ki = ki_ref[0:lk, :]
    score = jnp.zeros((lk, qb), F32)
    for h in range(0, IDX_HEADS, HEADS_PER_DOT):
        qi = qih_ref[h:h + HEADS_PER_DOT].reshape(HEADS_PER_DOT * qb, IDX_DIM)
        s = jnp.maximum(_dot_nt(ki, qi), 0.0)
        for r in range(HEADS_PER_DOT):
            score = score + s[:, r * qb:(r + 1) * qb] * wiT_ref[h + r:h + r + 1, :]
    score = jnp.where(visible, score, -jnp.inf)

    def count(mask):
        return _col_sum(jnp.where(mask, 1, 0).astype(I32))

    def bit_step(i, ans):
        trial = jnp.where(i == 0, jnp.zeros_like(ans), ans | jnp.left_shift(1, 31 - i))
        return jnp.where(count(score >= _key_to_float(trial)) >= topk, trial, ans)

    thr = _key_to_float(lax.fori_loop(0, 32, bit_step, jnp.full((1, qb), INT_MIN, I32)))
    above = score > thr
    equal = score == thr
    need = topk - count(above)
    over = count(equal) > need
    few = (j * qb + lax.broadcasted_iota(I32, (1, qb), 1)) < topk

    def tie_cut():
        nbits = max(lk - 1, 1).bit_length()

        def idx_step(i, ans):
            trial = ans | jnp.left_shift(1, nbits - 1 - i)
            c = count(jnp.logical_and(equal, kidx < trial))
            return jnp.where(c < need, trial, ans)

        cut = lax.fori_loop(0, nbits, idx_step, jnp.zeros((1, qb), I32))
        return jnp.where(over, cut, INT_MAX)

    cut = lax.cond(jnp.max(over.astype(I32)) > 0, tie_cut, lambda: jnp.full((1, qb), INT_MAX, I32))
    picked = jnp.logical_or(above, jnp.logical_and(equal, kidx <= cut))
    sel = jnp.logical_and(visible, jnp.logical_or(few, picked))
    bias = jnp.where(sel, 0.0, NEG_BIG)

    k = k_ref[0:lk, :]
    vT = vT_ref[:, 0:lk]
    bias2 = jnp.concatenate([bias] * HEADS_PER_DOT, axis=1)
    for h in range(0, ATT_HEADS, HEADS_PER_DOT):
        qq = qh_ref[h:h + HEADS_PER_DOT].reshape(HEADS_PER_DOT * qb, ATT_DIM)
        s = _dot_nt(k, qq) + bias2
        m = _col_max(s)
        p = jnp.exp2(s - m).astype(BF16)
        ov = _dot(vT, p)
        oT = ov[0:ATT_DIM] / ov[ATT_DIM:ATT_DIM + 1]
        for r in range(HEADS_PER_DOT):
            o_ref[:, (h + r) * ATT_DIM:(h + r + 1) * ATT_DIM] = (
                jnp.transpose(oT[:, r * qb:(r + 1) * qb]).astype(o_ref.dtype))


def _dsa_kernel(qh_ref, qih_ref, wiT_ref, k_ref, vT_ref, ki_ref, o_ref, *, seq):
    j = pl.program_id(1)
    per_group = DSA_KGROUP // Q_BLOCK
    topk = min(TOPK, seq // 4)
    for g in range(seq // DSA_KGROUP):
        @pl.when(j // per_group == g)
        def _(g=g):
            _dsa_body((g + 1) * DSA_KGROUP, topk, j, qh_ref, qih_ref, wiT_ref, k_ref, vT_ref, ki_ref, o_ref)


def _dsa_attention(qh, qih, wiT, k_r, vT, ki_r, bsz, seq):
    t = k_r.shape[0]
    nb = seq // Q_BLOCK
    return pl.pallas_call(
        functools.partial(_dsa_kernel, seq=seq),
        out_shape=jax.ShapeDtypeStruct((t, ATT_D), BF16),
        grid=(bsz, nb),
        in_specs=[
            pl.BlockSpec((ATT_HEADS, Q_BLOCK, ATT_DIM), lambda b, j: (0, b * nb + j, 0)),
            pl.BlockSpec((IDX_HEADS, Q_BLOCK, IDX_DIM), lambda b, j: (0, b * nb + j, 0)),
            pl.BlockSpec((IDX_HEADS, Q_BLOCK), lambda b, j: (0, b * nb + j)),
            pl.BlockSpec((seq, ATT_DIM), lambda b, j: (b, 0)),
            pl.BlockSpec((None, VT_ROWS, seq), lambda b, j: (b, 0, 0)),
            pl.BlockSpec((seq, IDX_DIM), lambda b, j: (b, 0)),
        ],
        out_specs=pl.BlockSpec((Q_BLOCK, ATT_D), lambda b, j: (b * nb + j, 0)),
        compiler_params=_cparams(("parallel", "parallel")),
        name="dsa_attention",
    )(qh, qih, wiT, k_r, vT, ki_r)


def _pad_rows(w, n):
    return jnp.pad(w, ((0, n - w.shape[0]), (0, 0)))


def _hybrid_mixer(x2, mods_l, norm_w_l, pos_col, w_in, e, conv_w, conv_b, a_log, dt_bias, d_skip, ssd_norm_w,
                  ret_norm_w, w_out, bsz, seq):
    h = _norm_mod(x2, norm_w_l[2], mods_l, 1, seq)
    tm = min(MM_TM, seq)
    n_zx = SSD_D + SSD_D + 2 * SSD_BCW
    ret_lo = n_zx + SSD_HEADS
    w_t = jnp.swapaxes(w_in, 1, 2)
    w_dt = _pad_rows(w_t[e, n_zx:ret_lo], LANES)
    proj_zx, dt_pad = _matmul_wt(h, w_t, BF16, tm, 1024, n_zx, e, side_w=w_dt)
    proj_ret = _matmul_wt(h, w_t, BF16, tm, 1024, w_t.shape[1] - ret_lo, e, row0=ret_lo)
    dtT = jnp.swapaxes(dt_pad[:, :SSD_HEADS].reshape(bsz, seq, SSD_HEADS), 1, 2)
    ya = _ssd(proj_zx, dt_pad, dtT, conv_w, conv_b, a_log, dt_bias, d_skip, ssd_norm_w, bsz, seq)
    yb = _retention(proj_ret, pos_col, ret_norm_w, bsz, seq)
    return _proj_resid([ya, yb], w_out[e].astype(BF16), x2, mods_l, 1, norm_w_l[3], seq)


def _dsa_mixer(x2, mods_l, norm_w_l, pos_col, w_in, o, knw, w_out, bsz, seq):
    h = _norm_mod(x2, norm_w_l[2], mods_l, 1, seq)
    tm = min(MM_TM, seq)
    n_main = ATT_D + 2 * ATT_DIM + IDX_D
    w_t = jnp.swapaxes(w_in, 1, 2)
    w_tail = _pad_rows(w_t[o, n_main:], LANES)
    proj1, tail = _matmul_wt(h, w_t, BF16, tm, n_main // 2, n_main, o, side_w=w_tail)
    qh, qih, k_r, vT, ki_r, wi = _dsa_prep(proj1, tail, pos_col, knw, bsz, seq)
    att = _dsa_attention(qh, qih, wi.T, k_r, vT, ki_r, bsz, seq)
    return _proj_resid([att], w_out[o].astype(BF16), x2, mods_l, 1, norm_w_l[3], seq)


def kernel(x, c, positions, mod_w, mod_b, norm_w, ffn_w_gate, ffn_w_up, ffn_w_down, hy_w_in, hy_conv_w,
           hy_conv_b, ssd_A_log, ssd_dt_bias, ssd_D, ssd_norm_w, ret_norm_w, hy_w_out, dsa_w_in,
           idx_k_norm_w, dsa_w_out):
    bsz, seq, d = x.shape
    depth = mod_w.shape[0]
    x2 = x.reshape(bsz * seq, d)
    pos_col = positions.reshape(bsz * seq, 1)
    mods = _mods(c, mod_w, mod_b)
    for i in range(depth):
        ml, nw = mods[i], norm_w[i]
        x2 = _ffn(x2, ml, 0, nw[0], nw[1], ffn_w_gate, ffn_w_up, ffn_w_down, i, 0, seq)
        if i % 2 == 0:
            e = i // 2
            x2 = _hybrid_mixer(x2, ml, nw, pos_col, hy_w_in, e, hy_conv_w[e], hy_conv_b[e], ssd_A_log[e],
                               ssd_dt_bias[e], ssd_D[e], ssd_norm_w[e], ret_norm_w[e], hy_w_out, bsz, seq)
        else:
            o = i // 2
            x2 = _dsa_mixer(x2, ml, nw, pos_col, dsa_w_in, o, idx_k_norm_w[o], dsa_w_out, bsz, seq)
        x2 = _ffn(x2, ml, 2, nw[4], nw[5], ffn_w_gate, ffn_w_up, ffn_w_down, i, 1, seq)
    return x2.reshape(bsz, seq, d)
```

```python
import functools
import math

import jax
import jax.numpy as jnp
from jax import lax
from jax.experimental import pallas as pl
from jax.experimental.pallas import tpu as pltpu

F32 = jnp.float32
BF16 = jnp.bfloat16
I32 = jnp.int32

EPS = 1e-6
SSD_HEADS = 32
SSD_HEAD_DIM = 64
SSD_GROUPS = 8
SSD_STATE = 128
SSD_CONV = 4
CHUNK = 128
RET_HEADS = 8
RET_DIM = 256
RET_THETA = 10000.0
ATT_HEADS = 16
ATT_DIM = 128
IDX_HEADS = 16
IDX_DIM = 64
TOPK = 256
Q_BLOCK = 128
ROPE_THETA = 500000.0
ROPE_FRACTION = 4

V7X_VMEM_BYTES = 64 * 1024 * 1024
VMEM_LIMIT = V7X_VMEM_BYTES - 8 * 1024 * 1024
LANES = 128
SUBLANES = 8

INT_MIN = -(2 ** 31)
INT_MAX = 2 ** 31 - 1
NEG_BIG = -1e30


def _cparams(sem):
    return pltpu.CompilerParams(dimension_semantics=sem, vmem_limit_bytes=VMEM_LIMIT)


def _sigmoid(x):
    return 1.0 / (1.0 + jnp.exp(-x))


def _silu(x):
    return x * _sigmoid(x)


def _softplus(x):
    return jnp.maximum(x, 0.0) + jnp.log1p(jnp.exp(-jnp.abs(x)))


def _rms(x, w):
    return x * lax.rsqrt(jnp.mean(x * x, axis=-1, keepdims=True) + EPS) * w


def _dot(a, b):
    return jnp.dot(a, b, preferred_element_type=F32)


def _dot_nt(a, b):
    return lax.dot_general(a, b, (((1,), (1,)), ((), ())), preferred_element_type=F32)


def _dot_tn(a, b):
    return lax.dot_general(a, b, (((0,), (0,)), ((), ())), preferred_element_type=F32)


def _split3(x):
    hi = x.astype(BF16)
    r = x - hi.astype(F32)
    mid = r.astype(BF16)
    lo = (r - mid.astype(F32)).astype(BF16)
    return hi, mid, lo


def _row_chunks(n_rows, rc, fn):
    def body(i, carry):
        fn(pl.ds(pl.multiple_of(i * rc, rc), rc))
        return carry
    lax.fori_loop(0, n_rows // rc, body, 0)


ROW_CHUNK = 128


def _inv_rms_rows(src_ref, r_ref, n_rows):
    def body(rows):
        x = src_ref[rows, :]
        r_ref[rows, :] = lax.rsqrt(jnp.mean(x * x, axis=-1, keepdims=True) + EPS)
    _row_chunks(n_rows, ROW_CHUNK, body)


def _mods_kernel(c_ref, w_ref, b_ref, o_ref):
    cond = _silu(c_ref[...]).astype(BF16)
    o_ref[...] = _dot(cond, w_ref[...].astype(BF16)) + b_ref[...]


def _mods(c, mod_w, mod_b):
    depth, d, n = mod_w.shape
    bsz = c.shape[0]
    c8 = jnp.zeros((SUBLANES, d), F32).at[:bsz].set(c)
    tn = 1024
    out = pl.pallas_call(
        _mods_kernel,
        out_shape=jax.ShapeDtypeStruct((depth, SUBLANES, n), F32),
        grid=(depth, n // tn),
        in_specs=[
            pl.BlockSpec((SUBLANES, d), lambda i, j: (0, 0)),
            pl.BlockSpec((None, d, tn), lambda i, j: (i, 0, j)),
            pl.BlockSpec((None, 1, tn), lambda i, j: (i, 0, j)),
        ],
        out_specs=pl.BlockSpec((None, SUBLANES, tn), lambda i, j: (i, 0, j)),
        compiler_params=_cparams(("parallel", "parallel")),
        name="mods",
    )(c8, mod_w, mod_b.reshape(depth, 1, n))
    return out[:, :bsz].reshape(depth, bsz * 3, 3, d)


NORM_TM = 512


def _norm_mod_kernel(x_ref, w_ref, m_ref, o_ref, r_ref):
    n_rows = x_ref.shape[0]
    _inv_rms_rows(x_ref, r_ref, n_rows)
    wmod = w_ref[...] * (1.0 + m_ref[1:2, :])
    shift = m_ref[0:1, :]

    def body(rows):
        o_ref[rows, :] = (x_ref[rows, :] * r_ref[rows, :] * wmod + shift).astype(o_ref.dtype)
    _row_chunks(n_rows, ROW_CHUNK, body)


def _norm_mod(x2, pre_w, mods_l, sub, seq):
    t, d = x2.shape
    tm = min(NORM_TM, seq)
    per_b = seq // tm
    return pl.pallas_call(
        _norm_mod_kernel,
        out_shape=jax.ShapeDtypeStruct((t, d), BF16),
        grid=(t // tm,),
        in_specs=[
            pl.BlockSpec((tm, d), lambda i: (i, 0)),
            pl.BlockSpec((1, d), lambda i: (0, 0)),
            pl.BlockSpec((None, 3, d), lambda i: ((i // per_b) * 3 + sub, 0, 0)),
        ],
        out_specs=pl.BlockSpec((tm, d), lambda i: (i, 0)),
        scratch_shapes=[pltpu.VMEM((tm, 1), F32)],
        compiler_params=_cparams(("parallel",)),
        name="norm_mod",
    )(x2, pre_w.reshape(1, d), mods_l)


MM_TM = 1024


def _mm_kernel(a_ref, w_ref, o_ref):
    o_ref[...] = _dot_nt(a_ref[...], w_ref[...].astype(BF16)).astype(o_ref.dtype)


def _mm_side_kernel(a_ref, w_ref, ws_ref, o_ref, os_ref):
    _mm_kernel(a_ref, w_ref, o_ref)

    @pl.when(pl.program_id(1) == 0)
    def _():
        os_ref[...] = _dot_nt(a_ref[...], ws_ref[...].astype(BF16))


def _matmul_wt(a, w_t, out_dtype, tm, tn, n_out, lead, row0=0, side_w=None):
    m, k = a.shape
    if row0 % tn == 0:
        w_spec = pl.BlockSpec((None, tn, k), lambda i, j: (lead, row0 // tn + j, 0))
    else:
        w_t = w_t[lead]
        assert row0 % SUBLANES == 0 and tn % SUBLANES == 0
        w_spec = pl.BlockSpec((pl.Element(tn), pl.Element(k)),
                              lambda i, j: ((row0 // SUBLANES + j * (tn // SUBLANES)) * SUBLANES, 0))
    a_spec = pl.BlockSpec((tm, k), lambda i, j: (i, 0))
    o_spec = pl.BlockSpec((tm, tn), lambda i, j: (i, j))
    if side_w is None:
        return pl.pallas_call(
            _mm_kernel,
            out_shape=jax.ShapeDtypeStruct((m, n_out), out_dtype),
            grid=(m // tm, n_out // tn),
            in_specs=[a_spec, w_spec],
            out_specs=o_spec,
            compiler_params=_cparams(("parallel", "parallel")),
            name="matmul",
        )(a, w_t)
    return pl.pallas_call(
        _mm_side_kernel,
        out_shape=(jax.ShapeDtypeStruct((m, n_out), out_dtype), jax.ShapeDtypeStruct((m, LANES), F32)),
        grid=(m // tm, n_out // tn),
        in_specs=[a_spec, w_spec, pl.BlockSpec((LANES, k), lambda i, j: (0, 0))],
        out_specs=(o_spec, pl.BlockSpec((tm, LANES), lambda i, j: (i, 0))),
        compiler_params=_cparams(("parallel", "arbitrary")),
        name="matmul_side",
    )(a, w_t, side_w)


FFN_TM = 1024
FFN_TF = 512
FFN_SUB = 256


def _ffn_kernel(x_ref, m_ref, prew_ref, postw_ref, wg_ref, wu_ref, wd_ref, o_ref, h_ref, r_ref, *, nf, tm, tf):
    f = pl.program_id(1)

    @pl.when(f == 0)
    def _():
        _inv_rms_rows(x_ref, r_ref, tm)
        wmod = prew_ref[...] * (1.0 + m_ref[1:2, :])
        shift = m_ref[0:1, :]

        def pro(rows):
            h_ref[rows, :] = (x_ref[rows, :] * r_ref[rows, :] * wmod + shift).astype(BF16)
            o_ref[rows, :] = jnp.zeros((ROW_CHUNK, o_ref.shape[1]), F32)
        _row_chunks(tm, ROW_CHUNK, pro)

    h = h_ref[...]
    acts = []
    for c in range(tf // FFN_SUB):
        cs = slice(c * FFN_SUB, (c + 1) * FFN_SUB)
        g = _dot(h, wg_ref[:, cs].astype(BF16))
        u = _dot(h, wu_ref[:, cs].astype(BF16))
        acts.append((_silu(g) * u).astype(BF16))
    o_ref[...] += _dot(jnp.concatenate(acts, axis=1), wd_ref[...].astype(BF16))

    @pl.when(f == nf - 1)
    def _():
        _inv_rms_rows(o_ref, r_ref, tm)
        gmod = postw_ref[...] * (0.5 * m_ref[2:3, :])

        def epi(rows):
            o_ref[rows, :] = x_ref[rows, :] + o_ref[rows, :] * r_ref[rows, :] * gmod
        _row_chunks(tm, ROW_CHUNK, epi)


def _ffn(x2, mods_l, sub, pre_w, post_w, w_gate, w_up, w_down, layer, which, seq):
    t, d = x2.shape
    dff = w_gate.shape[-1]
    tm, tf = min(FFN_TM, seq), FFN_TF
    nf = dff // tf
    per_b = seq // tm
    one = pl.Buffered(1)
    return pl.pallas_call(
        functools.partial(_ffn_kernel, nf=nf, tm=tm, tf=tf),
        out_shape=jax.ShapeDtypeStruct((t, d), F32),
        grid=(t // tm, nf),
        in_specs=[
            pl.BlockSpec((tm, d), lambda i, f: (i, 0), pipeline_mode=one),
            pl.BlockSpec((None, 3, d), lambda i, f: ((i // per_b) * 3 + sub, 0, 0)),
            pl.BlockSpec((1, d), lambda i, f: (0, 0)),
            pl.BlockSpec((1, d), lambda i, f: (0, 0)),
            pl.BlockSpec((None, None, d, tf), lambda i, f: (layer, which, 0, f)),
            pl.BlockSpec((None, None, d, tf), lambda i, f: (layer, which, 0, f)),
            pl.BlockSpec((None, None, tf, d), lambda i, f: (layer, which, f, 0)),
        ],
        out_specs=pl.BlockSpec((tm, d), lambda i, f: (i, 0), pipeline_mode=one),
        scratch_shapes=[pltpu.VMEM((tm, d), BF16), pltpu.VMEM((tm, 1), F32)],
        compiler_params=_cparams(("parallel", "arbitrary")),
        name="ffn",
    )(x2, mods_l, pre_w.reshape(1, d), post_w.reshape(1, d), w_gate, w_up, w_down)


PROJ_TM = 512


def _proj_resid_kernel(*refs, n_src, tm):
    a_refs = refs[:n_src]
    w_ref, x_ref, m_ref, postw_ref, o_ref, r_ref = refs[n_src:]
    ks = a_refs[0].shape[1]
    acc = _dot(a_refs[0][...], w_ref[0:ks, :])
    for s in range(1, n_src):
        acc = acc + _dot(a_refs[s][...], w_ref[s * ks:(s + 1) * ks, :])
    o_ref[...] = acc

    _inv_rms_rows(o_ref, r_ref, tm)
    gmod = postw_ref[...] * m_ref[2:3, :]

    def epi(rows):
        o_ref[rows, :] = x_ref[rows, :] + o_ref[rows, :] * r_ref[rows, :] * gmod
    _row_chunks(tm, ROW_CHUNK, epi)


def _proj_resid(srcs, w_bf, x2, mods_l, sub, post_w, seq):
    t, d = x2.shape
    n_src = len(srcs)
    ks = srcs[0].shape[1]
    tm = min(PROJ_TM, seq)
    per_b = seq // tm
    return pl.pallas_call(
        functools.partial(_proj_resid_kernel, n_src=n_src, tm=tm),
        out_shape=jax.ShapeDtypeStruct((t, d), F32),
        grid=(t // tm,),
        in_specs=[pl.BlockSpec((tm, ks), lambda i: (i, 0)) for _ in range(n_src)] + [
            pl.BlockSpec((n_src * ks, d), lambda i: (0, 0), pipeline_mode=pl.Buffered(1)),
            pl.BlockSpec((tm, d), lambda i: (i, 0)),
            pl.BlockSpec((None, 3, d), lambda i: ((i // per_b) * 3 + sub, 0, 0)),
            pl.BlockSpec((1, d), lambda i: (0, 0)),
        ],
        out_specs=pl.BlockSpec((tm, d), lambda i: (i, 0)),
        scratch_shapes=[pltpu.VMEM((tm, 1), F32)],
        compiler_params=_cparams(("parallel",)),
        name="proj_resid",
    )(*srcs, w_bf, x2, mods_l, post_w.reshape(1, d))


SSD_D = SSD_HEADS * SSD_HEAD_DIM
SSD_GW = SSD_D // SSD_GROUPS
SSD_BCW = SSD_GROUPS * SSD_STATE


def _ssd_kernel(z_ref, xs_ref, bc_ref, xsp_ref, bcp_ref, dt_ref, dtT_ref, cw_ref, cb_ref, alr_ref, alc_ref,
                dbr_ref, dbc_ref, dexp_ref, nw_ref, e_ref, o_ref, s_ref):
    c = pl.program_id(1)
    q = CHUNK

    @pl.when(c == 0)
    def _():
        s_ref[...] = jnp.zeros_like(s_ref)

    n_shift = SSD_CONV - 1
    tt = lax.broadcasted_iota(I32, (n_shift * q, 2 * q), 0)
    rr = lax.broadcasted_iota(I32, (n_shift * q, 2 * q), 1)
    shift_mat = (rr == (tt % q) + q - n_shift + tt // q).astype(BF16)

    def conv(cur_ref, prev_ref, lo, hi):
        cur = cur_ref[...]
        prev = jnp.where(c > 0, prev_ref[...], jnp.zeros_like(cur))
        delayed = _dot(shift_mat, jnp.concatenate([prev, cur], axis=0))
        acc = cb_ref[:, lo:hi] + cw_ref[n_shift:n_shift + 1, lo:hi] * cur.astype(F32)
        for k in range(n_shift):
            acc = acc + cw_ref[k:k + 1, lo:hi] * delayed[k * q:(k + 1) * q]
        return _silu(acc)

    xs = conv(xs_ref, xsp_ref, 0, SSD_D)
    bc = conv(bc_ref, bcp_ref, SSD_D, SSD_D + 2 * SSD_BCW)

    rows = lax.broadcasted_iota(I32, (q, q), 0)
    cols = lax.broadcasted_iota(I32, (q, q), 1)
    causal = rows >= cols
    tril = causal.astype(BF16)
    triu = (rows <= cols).astype(BF16)

    dt = _softplus(dt_ref[:, 0:SSD_HEADS] + dbr_ref[...])
    dA = dt * (-jnp.exp(alr_ref[...]))
    cum = sum(_dot(tril, p) for p in _split3(dA))
    dtT = _softplus(dtT_ref[...] + dbc_ref[...])
    dAT = dtT * (-jnp.exp(alc_ref[...]))
    cumT = sum(_dot(p, triu) for p in _split3(dAT))

    ecum = jnp.exp(cum)
    decs = jnp.exp(cum[q - 1:q, :] - cum)
    stack = jnp.concatenate([dt, ecum, decs], axis=0)
    ex = sum(_dot(p, e_ref[...]) for p in _split3(stack))
    dt_x, ecum_x, decs_x = ex[0:q], ex[q:2 * q], ex[2 * q:3 * q]
    cdec_x = ecum_x[q - 1:q, :]

    xdt = xs * dt_x
    xw = (xdt * decs_x).astype(BF16)
    xdt_b = xdt.astype(BF16)

    y_parts = []
    for g in range(SSD_GROUPS):
        b_g = bc[:, g * SSD_STATE:(g + 1) * SSD_STATE].astype(BF16)
        c_g = bc[:, SSD_BCW + g * SSD_STATE:SSD_BCW + (g + 1) * SSD_STATE].astype(BF16)
        gs = slice(g * SSD_GW, (g + 1) * SSD_GW)
        cb = _dot_nt(c_g, b_g)
        s_g = s_ref[g]
        y_g = _dot(c_g, s_g.astype(BF16)) * ecum_x[:, gs]
        ms, xs_blocks = [], []
        xdt_g = xdt_b[:, gs]
        head_of_col = lax.broadcasted_iota(I32, (q, SSD_GW), 1) // SSD_HEAD_DIM
        for r in range(SSD_HEADS // SSD_GROUPS):
            hh = g * (SSD_HEADS // SSD_GROUPS) + r
            seg = cum[:, hh:hh + 1] - cumT[hh:hh + 1, :]
            dec = jnp.where(causal, jnp.exp(seg), 0.0)
            ms.append((cb * dec).astype(BF16))
            xs_blocks.append(jnp.where(head_of_col == r, xdt_g, jnp.zeros_like(xdt_g)))
        y_parts.append(y_g + _dot(jnp.concatenate(ms, axis=1), jnp.concatenate(xs_blocks, axis=0)))
        s_ref[g] = s_g * cdec_x[:, gs] + _dot_tn(b_g, xw[:, gs])
    y = jnp.concatenate(y_parts, axis=1) + xs * dexp_ref[...]
    o_ref[...] = _rms(y * _silu(z_ref[...].astype(F32)), nw_ref[...]).astype(o_ref.dtype)


def _ssd(proj0, dt_pad, dtT, conv_w, conv_b, a_log, dt_bias, d_skip, norm_w, bsz, seq):
    t = proj0.shape[0]
    nc = seq // CHUNK
    h = SSD_HEADS
    expand = jnp.repeat(jnp.eye(h, dtype=BF16), SSD_HEAD_DIM, axis=1)
    dexp = jnp.repeat(d_skip.astype(F32), SSD_HEAD_DIM).reshape(1, SSD_D)
    full = lambda shape: pl.BlockSpec(shape, lambda b, c: (0,) * len(shape))
    col = lambda j: pl.BlockSpec((CHUNK, SSD_D), lambda b, c: (b * nc + c, j))
    prev = lambda j: pl.BlockSpec((CHUNK, SSD_D), lambda b, c: (b * nc + jnp.maximum(c - 1, 0), j))
    return pl.pallas_call(
        _ssd_kernel,
        out_shape=jax.ShapeDtypeStruct((t, SSD_D), BF16),
        grid=(bsz, nc),
        in_specs=[
            col(0), col(1), col(2), prev(1), prev(2),
            pl.BlockSpec((CHUNK, LANES), lambda b, c: (b * nc + c, 0)),
            pl.BlockSpec((None, h, CHUNK), lambda b, c: (b, 0, c)),
            full((SSD_CONV, SSD_D + 2 * SSD_BCW)), full((1, SSD_D + 2 * SSD_BCW)),
            full((1, h)), full((h, 1)), full((1, h)), full((h, 1)),
            full((1, SSD_D)), full((1, SSD_D)), full((h, SSD_D)),
        ],
        out_specs=pl.BlockSpec((CHUNK, SSD_D), lambda b, c: (b * nc + c, 0)),
        scratch_shapes=[pltpu.VMEM((SSD_GROUPS, SSD_STATE, SSD_GW), F32)],
        compiler_params=_cparams(("parallel", "arbitrary")),
        name="ssd",
    )(proj0, proj0, proj0, proj0, proj0, dt_pad, dtT, conv_w, conv_b.reshape(1, -1),
      a_log.reshape(1, h), a_log.reshape(h, 1), dt_bias.reshape(1, h), dt_bias.reshape(h, 1),
      dexp, norm_w.reshape(1, SSD_D), expand)


RET_D = RET_HEADS * RET_DIM


def _ret_kernel(q_ref, k_ref, v_ref, g_ref, pos_ref, inv_ref, nw_ref, o_ref, r_ref):
    c = pl.program_id(1)
    q = CHUNK
    half = RET_DIM // 2

    @pl.when(c == 0)
    def _():
        r_ref[...] = jnp.zeros_like(r_ref)

    ang = pos_ref[...].astype(F32) * inv_ref[...]
    cs, sn = jnp.cos(ang), jnp.sin(ang)
    li = lax.broadcasted_iota(I32, (q, q), 0)
    si = lax.broadcasted_iota(I32, (q, q), 1)
    dist = (li - si).astype(F32)
    lcol = lax.broadcasted_iota(I32, (q, 1), 0).astype(F32)

    def rope(x):
        x1, x2 = x[:, :half], x[:, half:]
        return jnp.concatenate([x1 * cs - x2 * sn, x1 * sn + x2 * cs], axis=1)

    for h in range(RET_HEADS):
        lg = math.log(1.0 - 2.0 ** (-5.0 - h))
        hs = slice(h * RET_DIM, (h + 1) * RET_DIM)
        qr = rope(q_ref[:, hs].astype(F32)).astype(BF16)
        kr = rope(k_ref[:, hs].astype(F32)) * (RET_DIM ** -0.5)
        v = v_ref[:, hs]
        intra = jnp.where(dist >= 0.0, jnp.exp(dist * lg), 0.0)
        sc = _dot_nt(qr, kr.astype(BF16)) * intra
        r_h = r_ref[h]
        y = _dot(sc.astype(BF16), v) + _dot(qr, r_h.astype(BF16)) * jnp.exp((lcol + 1.0) * lg)
        kd = (kr * jnp.exp((q - 1.0 - lcol) * lg)).astype(BF16)
        r_ref[h] = r_h * math.exp(q * lg) + _dot_tn(kd, v)
        gate = _silu(g_ref[:, hs].astype(F32))
        o_ref[:, hs] = (gate * _rms(y, nw_ref[:, hs])).astype(o_ref.dtype)


def _retention(proj0, pos_col, norm_w, bsz, seq):
    t = proj0.shape[0]
    nc = seq // CHUNK
    half = RET_DIM // 2
    inv = (RET_THETA ** (-jnp.arange(half, dtype=F32) / half)).reshape(1, half)
    col = lambda j: pl.BlockSpec((CHUNK, RET_D), lambda b, c: (b * nc + c, j))
    return pl.pallas_call(
        _ret_kernel,
        out_shape=jax.ShapeDtypeStruct((t, RET_D), BF16),
        grid=(bsz, nc),
        in_specs=[
            col(0), col(1), col(2), col(3),
            pl.BlockSpec((CHUNK, 1), lambda b, c: (b * nc + c, 0)),
            pl.BlockSpec((1, half), lambda b, c: (0, 0)),
            pl.BlockSpec((1, RET_D), lambda b, c: (0, 0)),
        ],
        out_specs=pl.BlockSpec((CHUNK, RET_D), lambda b, c: (b * nc + c, 0)),
        scratch_shapes=[pltpu.VMEM((RET_HEADS, RET_DIM, RET_DIM), F32)],
        compiler_params=_cparams(("parallel", "arbitrary")),
        name="retention",
    )(proj0, proj0, proj0, proj0, pos_col, inv, norm_w.reshape(1, RET_D))


ATT_D = ATT_HEADS * ATT_DIM
IDX_D = IDX_HEADS * IDX_DIM
ATT_ROT = ATT_DIM // ROPE_FRACTION
IDX_ROT = IDX_DIM // ROPE_FRACTION
PREP_TM = 256
VT_ROWS = ATT_DIM + 16
Q_SCALE = ATT_DIM ** -0.5 * math.log2(math.e)


def _rope_factors(ang, half, period):
    lane = lax.broadcasted_iota(I32, ang.shape, 1) % period
    first = lane < half
    second = jnp.logical_and(lane >= half, lane < 2 * half)
    cs, sn = jnp.cos(ang), jnp.sin(ang)
    cfac = jnp.where(jnp.logical_or(first, second), cs, 1.0)
    sfac = jnp.where(first, -sn, jnp.where(second, sn, 0.0))
    return first, cfac, sfac, half


def _partial_rope(x, factors):
    first, cfac, sfac, half = factors
    partner = jnp.where(first, pltpu.roll(x, LANES - half, axis=1), pltpu.roll(x, half, axis=1))
    return x * cfac + partner * sfac


def _dsa_prep_kernel(p_ref, tail_ref, pos_ref, inva_ref, invi_ref, knw_ref,
                     qh_ref, qih_ref, k_ref, vT_ref, ki_ref, wi_ref):
    pos = pos_ref[...].astype(F32)
    rope_a = _rope_factors(pos * inva_ref[...], ATT_ROT // 2, ATT_DIM)
    rope_i = _rope_factors(pos * invi_ref[...], IDX_ROT // 2, IDX_DIM)
    first_a, cfac_a, sfac_a, half_a = rope_a
    rope_q = (first_a, cfac_a * Q_SCALE, sfac_a * Q_SCALE, half_a)
    o_k, o_v, o_qi = ATT_D, ATT_D + ATT_DIM, ATT_D + 2 * ATT_DIM

    for h in range(ATT_HEADS):
        x = p_ref[:, h * ATT_DIM:(h + 1) * ATT_DIM].astype(F32)
        qh_ref[h] = _partial_rope(x, rope_q).astype(BF16)
    for p in range(IDX_HEADS // 2):
        x = p_ref[:, o_qi + p * LANES:o_qi + (p + 1) * LANES].astype(F32)
        y = _partial_rope(x, rope_i).astype(BF16)
        qih_ref[2 * p] = y[:, :IDX_DIM]
        qih_ref[2 * p + 1] = y[:, IDX_DIM:]

    k = p_ref[:, o_k:o_v].astype(F32)
    k_ref[...] = _partial_rope(k, rope_a).astype(BF16)
    vT_ref[0:ATT_DIM, :] = jnp.transpose(p_ref[:, o_v:o_qi].astype(F32)).astype(BF16)
    ones_row = lax.broadcasted_iota(I32, (VT_ROWS - ATT_DIM, vT_ref.shape[1]), 0) == 0
    vT_ref[ATT_DIM:VT_ROWS, :] = jnp.where(ones_row, 1.0, 0.0).astype(BF16)

    tail = tail_ref[...]
    lane = lax.broadcasted_iota(I32, tail.shape, 1)
    ki = jnp.where(lane < IDX_DIM, tail, 0.0)
    ms = jnp.sum(ki * ki, axis=-1, keepdims=True) * (1.0 / IDX_DIM)
    kin = ki * lax.rsqrt(ms + EPS) * knw_ref[...]
    ki_ref[...] = _partial_rope(kin, rope_i)[:, :IDX_DIM].astype(BF16)
    wi_ref[...] = tail[:, IDX_DIM:IDX_DIM + IDX_HEADS] * (IDX_HEADS ** -0.5 * IDX_DIM ** -0.5)


def _dsa_prep(proj1, tail, pos_col, knw, bsz, seq):
    t = proj1.shape[0]
    tm = PREP_TM
    per_b = seq // tm
    ha, hi = ATT_ROT // 2, IDX_ROT // 2
    inv_a = ROPE_THETA ** (-jnp.arange(ha, dtype=F32) / ha)
    inv_i = ROPE_THETA ** (-jnp.arange(hi, dtype=F32) / hi)
    lane = jnp.arange(LANES)
    inva = jnp.where(lane < ATT_ROT, inv_a[lane % ha], 0.0).reshape(1, LANES)
    invi = jnp.where(lane % IDX_DIM < IDX_ROT, inv_i[lane % hi], 0.0).reshape(1, LANES)
    knw_pad = jnp.zeros((1, LANES), F32).at[0, :IDX_DIM].set(knw)
    return pl.pallas_call(
        _dsa_prep_kernel,
        out_shape=(
            jax.ShapeDtypeStruct((ATT_HEADS, t, ATT_DIM), BF16),
            jax.ShapeDtypeStruct((IDX_HEADS, t, IDX_DIM), BF16),
            jax.ShapeDtypeStruct((t, ATT_DIM), BF16),
            jax.ShapeDtypeStruct((bsz, VT_ROWS, seq), BF16),
            jax.ShapeDtypeStruct((t, IDX_DIM), BF16),
            jax.ShapeDtypeStruct((t, IDX_HEADS), F32),
        ),
        grid=(t // tm,),
        in_specs=[
            pl.BlockSpec((tm, proj1.shape[1]), lambda i: (i, 0)),
            pl.BlockSpec((tm, LANES), lambda i: (i, 0)),
            pl.BlockSpec((tm, 1), lambda i: (i, 0)),
            pl.BlockSpec((1, LANES), lambda i: (0, 0)),
            pl.BlockSpec((1, LANES), lambda i: (0, 0)),
            pl.BlockSpec((1, LANES), lambda i: (0, 0)),
        ],
        out_specs=(
            pl.BlockSpec((ATT_HEADS, tm, ATT_DIM), lambda i: (0, i, 0)),
            pl.BlockSpec((IDX_HEADS, tm, IDX_DIM), lambda i: (0, i, 0)),
            pl.BlockSpec((tm, ATT_DIM), lambda i: (i, 0)),
            pl.BlockSpec((None, VT_ROWS, tm), lambda i: (i // per_b, 0, i % per_b)),
            pl.BlockSpec((tm, IDX_DIM), lambda i: (i, 0)),
            pl.BlockSpec((tm, IDX_HEADS), lambda i: (i, 0)),
        ),
        compiler_params=_cparams(("parallel",)),
        name="dsa_prep",
    )(proj1, tail, pos_col, inva, invi, knw_pad)


DSA_KGROUP = 512
HEADS_PER_DOT = 8


REDUCE_SLABS = 8


def _col_reduce(x, reduce_fn, combine_fn):
    n = x.shape[0] // REDUCE_SLABS
    parts = [reduce_fn(x[g * n:(g + 1) * n], axis=0, keepdims=True) for g in range(REDUCE_SLABS)]
    while len(parts) > 1:
        parts = [combine_fn(parts[i], parts[i + 1]) for i in range(0, len(parts), 2)]
    return parts[0]


def _col_sum(x):
    return _col_reduce(x, jnp.sum, jnp.add)


def _col_max(x):
    return _col_reduce(x, jnp.max, jnp.maximum)


def _key_to_float(key):
    return pltpu.bitcast(jnp.where(key < 0, key ^ INT_MAX, key), F32)


def _dsa_body(lk, topk, j, qh_ref, qih_ref, wiT_ref, k_ref, vT_ref, ki_ref, o_ref):
    qb = Q_BLOCK
    kidx = lax.broadcasted_iota(I32, (lk, qb), 0)
    qpos = j * qb + lax.broadcasted_iota(I32, (lk, qb), 1)
    visible = kidx <= qpos

    ki = ki_ref[0:lk, :]
    score = jnp.zeros((lk, qb), F32)
    for h in range(0, IDX_HEADS, HEADS_PER_DOT):
        qi = qih_ref[h:h + HEADS_PER_DOT].reshape(HEADS_PER_DOT * qb, IDX_DIM)
        s = jnp.maximum(_dot_nt(ki, qi), 0.0)
        for r in range(HEADS_PER_DOT):
            score = score + s[:, r * qb:(r + 1) * qb] * wiT_ref[h + r:h + r + 1, :]
    score = jnp.where(visible, score, -jnp.inf)

    def count(mask):
        return _col_sum(jnp.where(mask, 1, 0).astype(I32))

    def bit_step(i, ans):
        trial = jnp.where(i == 0, jnp.zeros_like(ans), ans | jnp.left_shift(1, 31 - i))
        return jnp.where(count(score >= _key_to_float(trial)) >= topk, trial, ans)

    thr = _key_to_float(lax.fori_loop(0, 32, bit_step, jnp.full((1, qb), INT_MIN, I32)))
    above = score > thr
    equal = score == thr
    need = topk - count(above)
    over = count(equal) > need
    few = (j * qb + lax.broadcasted_iota(I32, (1, qb), 1)) < topk

    def tie_cut():
        nbits = max(lk - 1, 1).bit_length()

        def idx_step(i, ans):
            trial = ans | jnp.left_shift(1, nbits - 1 - i)
            c = count(jnp.logical_and(equal, kidx < trial))
            return jnp.where(c < need, trial, ans)

        cut = lax.fori_loop(0, nbits, idx_step, jnp.zeros((1, qb), I32))
        return jnp.where(over, cut, INT_MAX)

    cut = lax.cond(jnp.max(over.astype(I32)) > 0, tie_cut, lambda: jnp.full((1, qb), INT_MAX, I32))
    picked = jnp.logical_or(above, jnp.logical_and(equal, kidx <= cut))
    sel = jnp.logical_and(visible, jnp.logical_or(few, picked))
    bias = jnp.where(sel, 0.0, NEG_BIG)

    k = k_ref[0:lk, :]
    vT = vT_ref[:, 0:lk]
    bias2 = jnp.concatenate([bias] * HEADS_PER_DOT, axis=1)
    for h in range(0, ATT_HEADS, HEADS_PER_DOT):
        qq = qh_ref[h:h + HEADS_PER_DOT].reshape(HEADS_PER_DOT * qb, ATT_DIM)
        s = _dot_nt(k, qq) + bias2
        m = _col_max(s)
        p = jnp.exp2(s - m).astype(BF16)
        ov = _dot(vT, p)
        oT = ov[0:ATT_DIM] / ov[ATT_DIM:ATT_DIM + 1]
        for r in range(HEADS_PER_DOT):
            o_ref[:, (h + r) * ATT_DIM:(h + r + 1) * ATT_DIM] = (
                jnp.transpose(oT[:, r * qb:(r + 1) * qb]).astype(o_ref.dtype))


def _dsa_kernel(qh_ref, qih_ref, wiT_ref, k_ref, vT_ref, ki_ref, o_ref, *, seq):
    j = pl.program_id(1)
    per_group = DSA_KGROUP // Q_BLOCK
    topk = min(TOPK, seq // 4)
    for g in range(seq // DSA_KGROUP):
        @pl.when(j // per_group == g)
        def _(g=g):
            _dsa_body((g + 1) * DSA_KGROUP, topk, j, qh_ref, qih_ref, wiT_ref, k_ref, vT_ref, ki_ref, o_ref)


def _dsa_attention(qh, qih, wiT, k_r, vT, ki_r, bsz, seq):
    t = k_r.shape[0]
    nb = seq // Q_BLOCK
    return pl.pallas_call(
        functools.partial(_dsa_kernel, seq=seq),
        out_shape=jax.ShapeDtypeStruct((t, ATT_D), BF16),
        grid=(bsz, nb),
        in_specs=[
            pl.BlockSpec((ATT_HEADS, Q_BLOCK, ATT_DIM), lambda b, j: (0, b * nb + j, 0)),
            pl.BlockSpec((IDX_HEADS, Q_BLOCK, IDX_DIM), lambda b, j: (0, b * nb + j, 0)),
            pl.BlockSpec((IDX_HEADS, Q_BLOCK), lambda b, j: (0, b * nb + j)),
            pl.BlockSpec((seq, ATT_DIM), lambda b, j: (b, 0)),
            pl.BlockSpec((None, VT_ROWS, seq), lambda b, j: (b, 0, 0)),
            pl.BlockSpec((seq, IDX_DIM), lambda b, j: (b, 0)),
        ],
        out_specs=pl.BlockSpec((Q_BLOCK, ATT_D), lambda b, j: (b * nb + j, 0)),
        compiler_params=_cparams(("parallel", "parallel")),
        name="dsa_attention",
    )(qh, qih, wiT, k_r, vT, ki_r)


def _pad_rows(w, n):
    return jnp.pad(w, ((0, n - w.shape[0]), (0, 0)))


def _hybrid_mixer(x2, mods_l, norm_w_l, pos_col, w_in, e, conv_w, conv_b, a_log, dt_bias, d_skip, ssd_norm_w,
                  ret_norm_w, w_out, bsz, seq):
    h = _norm_mod(x2, norm_w_l[2], mods_l, 1, seq)
    tm = min(MM_TM, seq)
    n_zx = SSD_D + SSD_D + 2 * SSD_BCW
    ret_lo = n_zx + SSD_HEADS
    w_t = jnp.swapaxes(w_in, 1, 2)
    w_dt = _pad_rows(w_t[e, n_zx:ret_lo], LANES)
    proj_zx, dt_pad = _matmul_wt(h, w_t, BF16, tm, 1024, n_zx, e, side_w=w_dt)
    proj_ret = _matmul_wt(h, w_t, BF16, tm, 1024, w_t.shape[1] - ret_lo, e, row0=ret_lo)
    dtT = jnp.swapaxes(dt_pad[:, :SSD_HEADS].reshape(bsz, seq, SSD_HEADS), 1, 2)
    ya = _ssd(proj_zx, dt_pad, dtT, conv_w, conv_b, a_log, dt_bias, d_skip, ssd_norm_w, bsz, seq)
    yb = _retention(proj_ret, pos_col, ret_norm_w, bsz, seq)
    return _proj_resid([ya, yb], w_out[e].astype(BF16), x2, mods_l, 1, norm_w_l[3], seq)


def _dsa_mixer(x2, mods_l, norm_w_l, pos_col, w_in, o, knw, w_out, bsz, seq):
    h = _norm_mod(x2, norm_w_l[2], mods_l, 1, seq)
    tm = min(MM_TM, seq)
    n_main = ATT_D + 2 * ATT_DIM + IDX_D
    w_t = jnp.swapaxes(w_in, 1, 2)
    w_tail = _pad_rows(w_t[o, n_main:], LANES)
    proj1, tail = _matmul_wt(h, w_t, BF16, tm, n_main // 2, n_main, o, side_w=w_tail)
    qh, qih, k_r, vT, ki_r, wi = _dsa_prep(proj1, tail, pos_col, knw, bsz, seq)
    att = _dsa_attention(qh, qih, wi.T, k_r, vT, ki_r, bsz, seq)
    return _proj_resid([att], w_out[o].astype(BF16), x2, mods_l, 1, norm_w_l[3], seq)


def kernel(x, c, positions, mod_w, mod_b, norm_w, ffn_w_gate, ffn_w_up, ffn_w_down, hy_w_in, hy_conv_w,
           hy_conv_b, ssd_A_log, ssd_dt_bias, ssd_D, ssd_norm_w, ret_norm_w, hy_w_out, dsa_w_in,
           idx_k_norm_w, dsa_w_out):
    bsz, seq, d = x.shape
    depth = mod_w.shape[0]
    x2 = x.reshape(bsz * seq, d)
    pos_col = positions.reshape(bsz * seq, 1)
    mods = _mods(c, mod_w, mod_b)
    for i in range(depth):
        ml, nw = mods[i], norm_w[i]
        x2 = _ffn(x2, ml, 0, nw[0], nw[1], ffn_w_gate, ffn_w_up, ffn_w_down, i, 0, seq)
        if i % 2 == 0:
            e = i // 2
            x2 = _hybrid_mixer(x2, ml, nw, pos_col, hy_w_in, e, hy_conv_w[e], hy_conv_b[e], ssd_A_log[e],
                               ssd_dt_bias[e], ssd_D[e], ssd_norm_w[e], ret_norm_w[e], hy_w_out, bsz, seq)
        else:
            o = i // 2
            x2 = _dsa_mixer(x2, ml, nw, pos_col, dsa_w_in, o, idx_k_norm_w[o], dsa_w_out, bsz, seq)
        x2 = _ffn(x2, ml, 2, nw[4], nw[5], ffn_w_gate, ffn_w_up, ffn_w_down, i, 1, seq)
    return x2.reshape(bsz, seq, d)
```
